```python
import jax
import jax.numpy as jnp
from jax import lax
import numpy as np

D_MODEL = 2048
BATCH = 4
SEQ = 2048
DEPTH = 4

GRID_W = 64
CTX_LEN = 256
D_FF = 5632
CONV_DIM = 1024
CONV_WIDTH = 31
GMLP_DIM = 1024
GMLP_CHUNK = 128
GMLP_GROUPS = 4
N_HEADS = 16
HEAD_DIM = 64
ATTN_DIM = N_HEADS * HEAD_DIM
WIN_ROWS = 8
WIN_COLS = 16
Q_BLOCK_COLS = 16
N_BRANCH = 3
N_MOD = 9
EPS = 1e-6
NEG_INF = -1e9

O_CONV = 0
O_GMLP = O_CONV + 2 * CONV_DIM
O_Q = O_GMLP + 2 * GMLP_DIM
O_K = O_Q + ATTN_DIM
O_V = O_K + ATTN_DIM
O_GATE = O_V + ATTN_DIM
IN_COLS = O_GATE + N_BRANCH * D_MODEL

kernel_name = 'hybrid_conv_gmlp_natten_prefix_trunk'


def rms_norm(x, g):
    xf = x.astype(jnp.float32)
    y = xf * lax.rsqrt(jnp.mean(xf * xf, axis=-1, keepdims=True) + EPS)
    return (y * g.astype(jnp.float32)).astype(x.dtype)


def layer_norm(x, g, b):
    xf = x.astype(jnp.float32)
    mu = jnp.mean(xf, axis=-1, keepdims=True)
    var = jnp.mean(jnp.square(xf - mu), axis=-1, keepdims=True)
    y = (xf - mu) * lax.rsqrt(var + EPS) * g.astype(jnp.float32) + b.astype(jnp.float32)
    return y.astype(x.dtype)


def modulate(h, shift, scale):
    return h * (1 + scale) + shift


def swiglu(h, w_gu, w_down):
    gu = h @ w_gu
    return (jax.nn.silu(gu[..., :D_FF]) * gu[..., D_FF:]) @ w_down


def ffn_sublayer(s, mod, j, norm_g, w_gu, w_down):
    h = modulate(rms_norm(s, norm_g), mod[3 * j], mod[3 * j + 1])
    return s + 0.5 * mod[3 * j + 2] * swiglu(h, w_gu, w_down)


def heads(t):
    return t.reshape(t.shape[0], t.shape[1], N_HEADS, HEAD_DIM)


def conv_module(a, dw, db, ln_g, ln_b):
    a = a[..., :CONV_DIM] * jax.nn.sigmoid(a[..., CONV_DIM:])
    y = lax.conv_general_dilated(
        a, dw[:, None, :].astype(a.dtype), window_strides=(1,),
        padding=[(CONV_WIDTH // 2, CONV_WIDTH // 2)],
        dimension_numbers=('NWC', 'WIO', 'NWC'), feature_group_count=CONV_DIM)
    return jax.nn.silu(layer_norm(y + db, ln_g, ln_b))


def gmlp_module(z, ln_g, ln_b, ws, bs):
    z = jax.nn.gelu(z)
    u, v = z[..., :GMLP_DIM], z[..., GMLP_DIM:]
    v = layer_norm(v, ln_g, ln_b)
    b, L, _ = v.shape
    v = v.reshape(b, L // GMLP_CHUNK, GMLP_CHUNK, GMLP_GROUPS, GMLP_DIM // GMLP_GROUPS)
    s = jnp.einsum('gpq,bnqgc->bnpgc', ws, v) + bs.T[None, None, :, :, None]
    return u * s.reshape(b, L, GMLP_DIM)


def neighbourhood_index(rows):
    kr = min(WIN_ROWS, rows)
    n_cb = GRID_W // Q_BLOCK_COLS
    kw = Q_BLOCK_COLS + WIN_COLS
    r = np.arange(rows)
    row_start = np.clip(r - kr // 2, 0, rows - kr)
    key_rows = row_start[:, None] + np.arange(kr)[None, :]
    qcol = np.arange(GRID_W).reshape(n_cb, Q_BLOCK_COLS)
    col_start = np.clip(qcol - WIN_COLS // 2, 0, GRID_W - WIN_COLS)
    blk_start = np.clip(qcol[:, 0] - WIN_COLS // 2, 0, GRID_W - kw)
    key_cols = blk_start[:, None] + np.arange(kw)[None, :]
    flat = key_rows[:, None, :, None] * GRID_W + key_cols[None, :, None, :]
    kc = key_cols[:, None, :]
    col_ok = (kc >= col_start[:, :, None]) & (kc < col_start[:, :, None] + WIN_COLS)
    row_off = key_rows - r[:, None] + WIN_ROWS - 1
    col_off = np.clip(kc - qcol[:, :, None], -(WIN_COLS - 1), WIN_COLS - 1) + WIN_COLS - 1
    return kr, kw, flat.reshape(-1), col_ok, row_off, col_off


def neighbourhood_attention(q, k, v, k_ctx, v_ctx, rpb):
    b, L, h, dh = q.shape
    rows = L // GRID_W
    n_cb = GRID_W // Q_BLOCK_COLS
    kr, kw, flat, col_ok, row_off, col_off = neighbourhood_index(rows)
    n_loc = kr * kw
    qb = q.reshape(b, rows, n_cb, Q_BLOCK_COLS, h, dh) * (dh ** -0.5)
    kg = jnp.take(k, flat, axis=1).reshape(b, rows, n_cb, n_loc, h, dh)
    vg = jnp.take(v, flat, axis=1).reshape(b, rows, n_cb, n_loc, h, dh)
    s_loc = jnp.einsum('brjqhd,brjkhd->bhrjqk', qb, kg).astype(jnp.float32)
    bias = rpb[:, row_off[:, None, None, :, None], col_off[None, :, :, None, :]].astype(jnp.float32)
    bias = jnp.where(col_ok[None, :, :, None, :], bias, NEG_INF)
    s_loc = s_loc + bias.reshape(h, rows, n_cb, Q_BLOCK_COLS, n_loc)[None]
    s_ctx = jnp.einsum('brjqhd,bkhd->bhrjqk', qb, k_ctx).astype(jnp.float32)
    p = jax.nn.softmax(jnp.concatenate([s_loc, s_ctx], axis=-1), axis=-1).astype(v.dtype)
    o = (jnp.einsum('bhrjqk,brjkhd->brjqhd', p[..., :n_loc], vg)
         + jnp.einsum('bhrjqk,bkhd->brjqhd', p[..., n_loc:], v_ctx))
    return o.reshape(b, L, h * dh)


def context_attention(q, k, v):
    b, L, h, dh = q.shape
    s = jnp.einsum('bqhd,bkhd->bhqk', q * (dh ** -0.5), k).astype(jnp.float32)
    p = jax.nn.softmax(s, axis=-1).astype(v.dtype)
    return jnp.einsum('bhqk,bkhd->bqhd', p, v).reshape(b, L, h * dh)


def mix_branches(p, attn, lp):
    conv = conv_module(p[..., O_CONV:O_GMLP], lp['conv_dw'], lp['conv_db'], lp['conv_ln_g'], lp['conv_ln_b'])
    gm = gmlp_module(p[..., O_GMLP:O_Q], lp['gmlp_ln_g'], lp['gmlp_ln_b'], lp['gmlp_ws'], lp['gmlp_bs'])
    g = jax.nn.sigmoid(p[..., O_GATE:].astype(jnp.float32)).astype(p.dtype)
    y = (g[..., :D_MODEL] * (conv @ lp['w_conv_out'])
         + g[..., D_MODEL:2 * D_MODEL] * (gm @ lp['w_gmlp_out'])
         + g[..., 2 * D_MODEL:] * (attn @ lp['w_attn_out']))
    return y @ lp['w_out']


def hybrid_layer(x, cx, mod_x, mod_c, lp, update_ctx):
    x = ffn_sublayer(x, mod_x, 0, lp['norm_ffn1'], lp['ffn1_w_gu'], lp['ffn1_w_down'])
    cx = ffn_sublayer(cx, mod_c, 0, lp['norm_ffn1'], lp['ffn1_w_gu'], lp['ffn1_w_down'])
    hx = modulate(rms_norm(x, lp['norm_mix']), mod_x[3], mod_x[4])
    hc = modulate(rms_norm(cx, lp['norm_mix']), mod_c[3], mod_c[4])
    px = hx @ lp['w_in']
    if update_ctx:
        pc = hc @ lp['w_in']
        kv_c = pc[..., O_K:O_GATE]
    else:
        kv_c = hc @ lp['w_in'][:, O_K:O_GATE]
    k_c = heads(kv_c[..., :ATTN_DIM])
    v_c = heads(kv_c[..., ATTN_DIM:])
    attn_x = neighbourhood_attention(heads(px[..., O_Q:O_K]), heads(px[..., O_K:O_V]),
                                     heads(px[..., O_V:O_GATE]), k_c, v_c, lp['attn_rpb'])
    x = x + mod_x[5] * mix_branches(px, attn_x, lp)
    x = ffn_sublayer(x, mod_x, 2, lp['norm_ffn2'], lp['ffn2_w_gu'], lp['ffn2_w_down'])
    if update_ctx:
        attn_c = context_attention(heads(pc[..., O_Q:O_K]), k_c, v_c)
        cx = cx + mod_c[5] * mix_branches(pc, attn_c, lp)
        cx = ffn_sublayer(cx, mod_c, 2, lp['norm_ffn2'], lp['ffn2_w_gu'], lp['ffn2_w_down'])
    return x, cx


def _normal(key, shape, scale):
    return jax.random.normal(key, shape, jnp.float32) * scale


def setup_inputs(seed: int = 0) -> dict:
    key = jax.random.key(seed)
    ks = jax.random.split(key, 28)
    D = D_MODEL
    return {
        'x': _normal(ks[0], (BATCH, SEQ, D), 1.0),
        'c': _normal(ks[1], (BATCH, D), 1.0),
        'ctx': _normal(ks[2], (BATCH, CTX_LEN, D), 1.0),
        'c_ctx': _normal(ks[3], (D,), 1.0),
        'w_mod': _normal(ks[4], (DEPTH, D, N_MOD * D), 0.5 * D ** -0.5),
        'b_mod': _normal(ks[5], (DEPTH, N_MOD * D), 0.01),
        'norm_ffn1': 1.0 + _normal(ks[6], (DEPTH, D), 0.01),
        'ffn1_w_gu': _normal(ks[7], (DEPTH, D, 2 * D_FF), D ** -0.5),
        'ffn1_w_down': _normal(ks[8], (DEPTH, D_FF, D), D_FF ** -0.5),
        'norm_mix': 1.0 + _normal(ks[9], (DEPTH, D), 0.01),
        'w_in': _normal(ks[10], (DEPTH, D, IN_COLS), D ** -0.5),
        'conv_dw': _normal(ks[11], (DEPTH, CONV_WIDTH, CONV_DIM), CONV_WIDTH ** -0.5),
        'conv_db': _normal(ks[12], (DEPTH, CONV_DIM), 0.01),
        'conv_ln_g': 1.0 + _normal(ks[13], (DEPTH, CONV_DIM), 0.01),
        'conv_ln_b': _normal(ks[14], (DEPTH, CONV_DIM), 0.01),
        'gmlp_ln_g': 1.0 + _normal(ks[15], (DEPTH, GMLP_DIM), 0.01),
        'gmlp_ln_b': _normal(ks[16], (DEPTH, GMLP_DIM), 0.01),
        'gmlp_ws': _normal(ks[17], (DEPTH, GMLP_GROUPS, GMLP_CHUNK, GMLP_CHUNK), GMLP_CHUNK ** -0.5),
        'gmlp_bs': 1.0 + _normal(ks[18], (DEPTH, GMLP_GROUPS, GMLP_CHUNK), 0.01),
        'attn_rpb': _normal(ks[19], (DEPTH, N_HEADS, 2 * WIN_ROWS - 1, 2 * WIN_COLS - 1), 0.1),
        'w_conv_out': _normal(ks[20], (DEPTH, CONV_DIM, D), CONV_DIM ** -0.5),
        'w_gmlp_out': _normal(ks[21], (DEPTH, GMLP_DIM, D), GMLP_DIM ** -0.5),
        'w_attn_out': _normal(ks[22], (DEPTH, ATTN_DIM, D), ATTN_DIM ** -0.5),
        'w_out': _normal(ks[23], (DEPTH, D, D), D ** -0.5),
        'norm_ffn2': 1.0 + _normal(ks[24], (DEPTH, D), 0.01),
        'ffn2_w_gu': _normal(ks[25], (DEPTH, D, 2 * D_FF), D ** -0.5),
        'ffn2_w_down': _normal(ks[26], (DEPTH, D_FF, D), D_FF ** -0.5),
        'final_norm': 1.0 + _normal(ks[27], (D,), 0.01),
    }


def reference(x, c, ctx, c_ctx, w_mod, b_mod, norm_ffn1, ffn1_w_gu, ffn1_w_down, norm_mix, w_in,
              conv_dw, conv_db, conv_ln_g, conv_ln_b, gmlp_ln_g, gmlp_ln_b, gmlp_ws, gmlp_bs,
              attn_rpb, w_conv_out, w_gmlp_out, w_attn_out, w_out, norm_ffn2, ffn2_w_gu,
              ffn2_w_down, final_norm):
    sc = jax.nn.silu(c)
    scc = jax.nn.silu(c_ctx)
    cx = ctx
    for i in range(DEPTH):
        mod_x = (sc @ w_mod[i] + b_mod[i]).reshape(-1, N_MOD, D_MODEL).transpose(1, 0, 2)[:, :, None, :]
        mod_c = (scc @ w_mod[i] + b_mod[i]).reshape(N_MOD, 1, 1, D_MODEL)
        lp = {
            'norm_ffn1': norm_ffn1[i], 'ffn1_w_gu': ffn1_w_gu[i], 'ffn1_w_down': ffn1_w_down[i],
            'norm_mix': norm_mix[i], 'w_in': w_in[i],
            'conv_dw': conv_dw[i], 'conv_db': conv_db[i], 'conv_ln_g': conv_ln_g[i], 'conv_ln_b': conv_ln_b[i],
            'gmlp_ln_g': gmlp_ln_g[i], 'gmlp_ln_b': gmlp_ln_b[i], 'gmlp_ws': gmlp_ws[i], 'gmlp_bs': gmlp_bs[i],
            'attn_rpb': attn_rpb[i], 'w_conv_out': w_conv_out[i], 'w_gmlp_out': w_gmlp_out[i],
            'w_attn_out': w_attn_out[i], 'w_out': w_out[i],
            'norm_ffn2': norm_ffn2[i], 'ffn2_w_gu': ffn2_w_gu[i], 'ffn2_w_down': ffn2_w_down[i],
        }
        x, cx = hybrid_layer(x, cx, mod_x, mod_c, lp, i < DEPTH - 1)
    return rms_norm(x, final_norm)
```

```python
import functools
import math

import numpy as np
import jax
import jax.numpy as jnp
from jax import lax
from jax.experimental import pallas as pl
from jax.experimental.pallas import tpu as pltpu

D_MODEL = 2048
D_FF = 5632
CONV_DIM = 1024
CONV_WIDTH = 31
GMLP_DIM = 1024
GMLP_CHUNK = 128
GMLP_GROUPS = 4
N_HEADS = 16
HEAD_DIM = 64
ATTN_DIM = N_HEADS * HEAD_DIM
GRID_W = 64
WIN_ROWS = 8
WIN_COLS = 16
N_MOD = 9
EPS = 1e-6
NEG_INF = -1e9

O_CONV = 0
O_GMLP = O_CONV + 2 * CONV_DIM
O_Q = O_GMLP + 2 * GMLP_DIM
O_GATE = O_Q + 3 * ATTN_DIM
IN_COLS = O_GATE + 3 * D_MODEL

MOD_ROWS = 8
V7X_LANES = 128
V7X_VMEM_BYTES = 64 * 1024 * 1024

F32 = jnp.float32
BF16 = jnp.bfloat16

TM_FFN = 512
TF_FFN = 512
TM_IN = 512
TN_IN = 1024
TM_OUT = 256
TQ_CONV = 256
HALO = 16
QB_ROWS = 4
QB = QB_ROWS * GRID_W
HP = 2 * HEAD_DIM


def _sigmoid(x):
    return 1.0 / (1.0 + jnp.exp(-x))


def _dot(a, b):
    return jnp.dot(a, b, preferred_element_type=F32)


def _dot_nt(a, b):
    return lax.dot_general(a, b, (((1,), (1,)), ((), ())), preferred_element_type=F32)


def _rms_mod(x, g, shift, scale):
    y = x * lax.rsqrt(jnp.mean(x * x, axis=-1, keepdims=True) + EPS)
    return (y * g) * (1.0 + scale) + shift


def _layer_norm(x, g, b):
    mu = jnp.mean(x, axis=-1, keepdims=True)
    xc = x - mu
    var = jnp.mean(xc * xc, axis=-1, keepdims=True)
    return xc * lax.rsqrt(var + EPS) * g + b


def _mod_row(i, tiles_per_batch, ctx_row):
    if tiles_per_batch is None:
        return ctx_row
    return i // tiles_per_batch


def _cparams(sem, vmem_mb):
    return pltpu.CompilerParams(dimension_semantics=sem, vmem_limit_bytes=vmem_mb * 1024 * 1024)


def _mod_kernel(c_ref, w_ref, b_ref, o_ref):
    c = c_ref[...]
    s = (c * _sigmoid(c)).astype(BF16)
    o_ref[...] = _dot(s, w_ref[...].astype(BF16)) + b_ref[...]


def _mod_table(cc, w_mod, b_mod):
    depth = w_mod.shape[0]
    tn = 1024
    n_cols = N_MOD * D_MODEL
    return pl.pallas_call(
        _mod_kernel,
        out_shape=jax.ShapeDtypeStruct((depth, MOD_ROWS, n_cols), F32),
        grid=(depth, n_cols // tn),
        in_specs=[
            pl.BlockSpec((MOD_ROWS, D_MODEL), lambda l, j: (0, 0)),
            pl.BlockSpec((None, D_MODEL, tn), lambda l, j: (l, 0, j)),
            pl.BlockSpec((None, 1, tn), lambda l, j: (l, 0, j)),
        ],
        out_specs=pl.BlockSpec((None, MOD_ROWS, tn), lambda l, j: (l, 0, j)),
        compiler_params=_cparams(("arbitrary", "arbitrary"), 40),
        name="mod_table",
    )(cc, w_mod, b_mod.reshape(depth, 1, n_cols))


def _ffn_kernel(tiles_per_batch, ctx_row, n_f, final, *refs):
    if final:
        (x_ref, sh_ref, sc_ref, gt_ref, g_ref, wg_ref, wu_ref, wd_ref, fg_ref,
         o_ref, h_ref, acc_ref) = refs
    else:
        (x_ref, sh_ref, sc_ref, gt_ref, g_ref, wg_ref, wu_ref, wd_ref,
         o_ref, h_ref, acc_ref) = refs
    i = pl.program_id(0)
    f = pl.program_id(1)
    row = _mod_row(i, tiles_per_batch, ctx_row)

    @pl.when(f == 0)
    def _():
        h = _rms_mod(x_ref[...], g_ref[...], sh_ref[pl.ds(row, 1), :], sc_ref[pl.ds(row, 1), :])
        h_ref[...] = h.astype(BF16)
        acc_ref[...] = jnp.zeros_like(acc_ref)

    h = h_ref[...]
    g = _dot(h, wg_ref[...])
    u = _dot(h, wu_ref[...])
    a = ((g * _sigmoid(g)) * u).astype(BF16)
    acc_ref[...] += _dot(a, wd_ref[...])

    @pl.when(f == n_f - 1)
    def _():
        y = x_ref[...] + (0.5 * gt_ref[pl.ds(row, 1), :]) * acc_ref[...]
        if final:
            y = y * lax.rsqrt(jnp.mean(y * y, axis=-1, keepdims=True) + EPS) * fg_ref[...]
        o_ref[...] = y


def _ffn(s, mod, layer, j, norm_g, w_gu, w_down, *, seq, ctx_row, final_g=None):
    rows = s.shape[0]
    tm, tf = TM_FFN, TF_FFN
    n_f = D_FF // tf
    tiles_per_batch = None if seq is None else seq // tm
    final = final_g is not None
    mod_spec = lambda k: pl.BlockSpec((None, MOD_ROWS, D_MODEL), lambda i, f: (layer, 0, 3 * j + k))
    in_specs = [
        pl.BlockSpec((tm, D_MODEL), lambda i, f: (i, 0)),
        mod_spec(0), mod_spec(1), mod_spec(2),
        pl.BlockSpec((None, 1, D_MODEL), lambda i, f: (layer, 0, 0)),
        pl.BlockSpec((None, D_MODEL, tf), lambda i, f: (layer, 0, f)),
        pl.BlockSpec((None, D_MODEL, tf), lambda i, f: (layer, 0, n_f + f)),
        pl.BlockSpec((None, tf, D_MODEL), lambda i, f: (layer, f, 0)),
    ]
    args = [s, mod, mod, mod, norm_g, w_gu, w_gu, w_down]
    if final:
        in_specs.append(pl.BlockSpec((1, D_MODEL), lambda i, f: (0, 0)))
        args.append(final_g)
    return pl.pallas_call(
        functools.partial(_ffn_kernel, tiles_per_batch, ctx_row, n_f, final),
        out_shape=jax.ShapeDtypeStruct((rows, D_MODEL), F32),
        grid=(rows // tm, n_f),
        in_specs=in_specs,
        out_specs=pl.BlockSpec((tm, D_MODEL), lambda i, f: (i, 0)),
        scratch_shapes=[pltpu.VMEM((tm, D_MODEL), BF16), pltpu.VMEM((tm, D_MODEL), F32)],
        compiler_params=_cparams(("arbitrary", "arbitrary"), 48),
        name="ffn",
    )(*args)


def _gelu_tanh(x):
    c = math.sqrt(2.0 / math.pi)
    return x * (0.5 * (1.0 + jnp.tanh(c * (x + 0.044715 * (x * x * x)))))


def _inproj_kernel(tiles_per_batch, ctx_row, x_ref, sh_ref, sc_ref, g_ref, w_ref, lng_ref, lnb_ref,
                   ws_ref, bst_ref, a_ref, gm_ref, qkv_ref, gate_ref, h_ref, pa_ref, u_ref):
    i = pl.program_id(0)
    j = pl.program_id(1)
    row = _mod_row(i, tiles_per_batch, ctx_row)
    tm = x_ref.shape[0]

    @pl.when(j == 0)
    def _():
        h = _rms_mod(x_ref[...], g_ref[...], sh_ref[pl.ds(row, 1), :], sc_ref[pl.ds(row, 1), :])
        h_ref[...] = h.astype(BF16)

    def proj():
        return _dot(h_ref[...], w_ref[...])

    @pl.when(j == 0)
    def _():
        pa_ref[...] = proj()

    @pl.when(j == 1)
    def _():
        a_ref[...] = (pa_ref[...] * _sigmoid(proj())).astype(BF16)

    @pl.when(j == 2)
    def _():
        u_ref[...] = _gelu_tanh(proj())

    @pl.when(j == 3)
    def _():
        vn = _layer_norm(_gelu_tanh(proj()), lng_ref[...], lnb_ref[...]).astype(BF16)
        gw = GMLP_DIM // GMLP_GROUPS
        for c in range(tm // GMLP_CHUNK):
            r0 = c * GMLP_CHUNK
            for g in range(GMLP_GROUPS):
                sp = _dot(ws_ref[g], vn[r0:r0 + GMLP_CHUNK, g * gw:(g + 1) * gw]) + bst_ref[:, g:g + 1]
                gm_ref[r0:r0 + GMLP_CHUNK, g * gw:(g + 1) * gw] = (
                    u_ref[r0:r0 + GMLP_CHUNK, g * gw:(g + 1) * gw] * sp).astype(BF16)

    @pl.when(j == 4)
    def _():
        qkv_ref[...] = (proj() * (HEAD_DIM ** -0.5)).astype(BF16)

    @pl.when((j == 5) | (j == 6))
    def _():
        qkv_ref[...] = proj().astype(BF16)

    @pl.when(j >= 7)
    def _():
        gate_ref[...] = _sigmoid(proj()).astype(BF16)


def _inproj(s, mod, layer, norm_g, w_in, ln_g, ln_b, ws, bst, *, seq, ctx_row):
    rows = s.shape[0]
    tm, tn = TM_IN, TN_IN
    n_j = IN_COLS // tn
    j_q, j_gate = O_Q // tn, O_GATE // tn
    tiles_per_batch = None if seq is None else seq // tm
    mod_spec = lambda k: pl.BlockSpec((None, MOD_ROWS, D_MODEL), lambda i, j: (layer, 0, 3 + k))
    vec = lambda n: pl.BlockSpec((None, 1, n), lambda i, j: (layer, 0, 0))
    return pl.pallas_call(
        functools.partial(_inproj_kernel, tiles_per_batch, ctx_row),
        out_shape=(
            jax.ShapeDtypeStruct((rows, CONV_DIM), BF16),
            jax.ShapeDtypeStruct((rows, GMLP_DIM), BF16),
            jax.ShapeDtypeStruct((rows, 3 * ATTN_DIM), BF16),
            jax.ShapeDtypeStruct((rows, 3 * D_MODEL), BF16),
        ),
        grid=(rows // tm, n_j),
        in_specs=[
            pl.BlockSpec((tm, D_MODEL), lambda i, j: (i, 0)),
            mod_spec(0), mod_spec(1),
            vec(D_MODEL),
            pl.BlockSpec((None, D_MODEL, tn), lambda i, j: (layer, 0, j)),
            vec(GMLP_DIM), vec(GMLP_DIM),
            pl.BlockSpec((None, GMLP_GROUPS, GMLP_CHUNK, GMLP_CHUNK), lambda i, j: (layer, 0, 0, 0)),
            pl.BlockSpec((None, GMLP_CHUNK, GMLP_GROUPS), lambda i, j: (layer, 0, 0)),
        ],
        out_specs=(
            pl.BlockSpec((tm, CONV_DIM), lambda i, j: (i, 0)),
            pl.BlockSpec((tm, GMLP_DIM), lambda i, j: (i, 0)),
            pl.BlockSpec((tm, tn), lambda i, j: (i, jnp.clip(j - j_q, 0, 2))),
            pl.BlockSpec((tm, tn), lambda i, j: (i, jnp.clip(j - j_gate, 0, n_j - j_gate - 1))),
        ),
        scratch_shapes=[pltpu.VMEM((tm, D_MODEL), BF16), pltpu.VMEM((tm, CONV_DIM), F32),
                        pltpu.VMEM((tm, GMLP_DIM), F32)],
        compiler_params=_cparams(("arbitrary", "arbitrary"), 48),
        name="inproj",
    )(s, mod, mod, norm_g, w_in, ln_g, ln_b, ws, bst)


def _conv_kernel(tiles_per_seq, prev_ref, cur_ref, next_ref, dw_ref, db_ref, lng_ref, lnb_ref,
                 o_ref, win_ref, sh_ref, acc_ref):
    tq = cur_ref.shape[0]
    pos = pl.program_id(0) % tiles_per_seq
    win_ref[0:HALO, :] = jnp.where(pos > 0, prev_ref[...].astype(F32), 0.0)
    win_ref[HALO:HALO + tq, :] = cur_ref[...].astype(F32)
    win_ref[HALO + tq:HALO + tq + HALO, :] = jnp.where(pos < tiles_per_seq - 1, next_ref[...].astype(F32), 0.0)
    n_sh = sh_ref.shape[1]
    for r in range(8):
        sh_ref[r] = win_ref[r:r + n_sh, :]
    rb = 64
    first = HALO - CONV_WIDTH // 2

    def body(t, carry):
        base = pl.multiple_of(t * rb, rb)
        for lb in range(CONV_DIM // V7X_LANES):
            lanes = slice(lb * V7X_LANES, (lb + 1) * V7X_LANES)
            acc = jnp.zeros((rb, V7X_LANES), F32)
            for k in range(CONV_WIDTH):
                off = first + k
                acc = acc + sh_ref[off % 8, pl.ds(base + (off // 8) * 8, rb), lanes] * dw_ref[k:k + 1, lanes]
            acc_ref[pl.ds(base, rb), lanes] = acc
        return carry

    lax.fori_loop(0, tq // rb, body, 0)
    y = _layer_norm(acc_ref[...] + db_ref[...], lng_ref[...], lnb_ref[...])
    o_ref[...] = (y * _sigmoid(y)).astype(BF16)


def _conv(a, layer, dw, db, ln_g, ln_b, *, seq):
    rows = a.shape[0]
    tq = TQ_CONV
    tiles_per_seq = seq // tq
    halo_blocks = rows // HALO
    per = tq // HALO
    vec = pl.BlockSpec((None, 1, CONV_DIM), lambda t: (layer, 0, 0))
    n_sh = tq + 2 * HALO - 8
    return pl.pallas_call(
        functools.partial(_conv_kernel, tiles_per_seq),
        out_shape=jax.ShapeDtypeStruct((rows, CONV_DIM), BF16),
        grid=(rows // tq,),
        in_specs=[
            pl.BlockSpec((HALO, CONV_DIM), lambda t: (jnp.maximum(t * per - 1, 0), 0)),
            pl.BlockSpec((tq, CONV_DIM), lambda t: (t, 0)),
            pl.BlockSpec((HALO, CONV_DIM), lambda t: (jnp.minimum((t + 1) * per, halo_blocks - 1), 0)),
            pl.BlockSpec((None, CONV_WIDTH, CONV_DIM), lambda t: (layer, 0, 0)),
            vec, vec, vec,
        ],
        out_specs=pl.BlockSpec((tq, CONV_DIM), lambda t: (t, 0)),
        scratch_shapes=[pltpu.VMEM((tq + 2 * HALO, CONV_DIM), F32),
                        pltpu.VMEM((8, n_sh, CONV_DIM), F32),
                        pltpu.VMEM((tq, CONV_DIM), F32)],
        compiler_params=_cparams(("arbitrary",), 40),
        name="conv",
    )(a, a, a, dw, db, ln_g, ln_b)


def _softmax_pv(scores, values):
    m = scores[0].max(axis=-1, keepdims=True)
    for s in scores[1:]:
        m = jnp.maximum(m, s.max(axis=-1, keepdims=True))
    l = None
    o = None
    for s, v in zip(scores, values):
        p = jnp.exp(s - m)
        ps = p.sum(axis=-1, keepdims=True)
        pv = _dot(p.astype(BF16), v)
        l = ps if l is None else l + ps
        o = pv if o is None else o + pv
    return o / l


def _nbr_attn_kernel(n_b, q_ref, k0_ref, k1_ref, k2_ref, v0_ref, v1_ref, v2_ref, kc_ref, vc_ref,
                     bias_ref, o_ref):
    lo = lax.broadcasted_iota(jnp.int32, (QB, HP), 1) < HEAD_DIM
    zero = jnp.zeros((QB, HP), BF16)

    def body(b, carry):
        q = q_ref[b]
        ks = [k0_ref[b], k1_ref[b], k2_ref[b], kc_ref[b]]
        vs = [v0_ref[b], v1_ref[b], v2_ref[b], vc_ref[b]]
        outs = []
        for hh in range(2):
            qm = jnp.where(lo, q, zero) if hh == 0 else jnp.where(lo, zero, q)
            scores = [_dot_nt(qm, kk) for kk in ks]
            for t in range(3):
                scores[t] = scores[t] + bias_ref[hh, 0, :, t * QB:(t + 1) * QB]
            outs.append(_softmax_pv(scores, vs))
        o_ref[b] = jnp.where(lo, outs[0], outs[1]).astype(BF16)
        return carry

    lax.fori_loop(0, n_b, body, 0)


def _nbr_attn(qkv, qkv_ctx, bias, layer):
    n_b, seq, _ = qkv.shape
    ctx = qkv_ctx.shape[1]
    n_g = seq // QB
    n_hp = ATTN_DIM // HP
    kb = lambda g: jnp.clip(g - 1, 0, n_g - 3)
    var = lambda g: jnp.where(g == 0, 0, jnp.where(g == n_g - 1, 2, 1))
    blk = (n_b, QB, HP)
    kspec = lambda t, col: pl.BlockSpec(blk, lambda hp, g: (0, kb(g) + t, col * n_hp + hp))
    return pl.pallas_call(
        functools.partial(_nbr_attn_kernel, n_b),
        out_shape=jax.ShapeDtypeStruct((n_b, seq, ATTN_DIM), BF16),
        grid=(n_hp, n_g),
        in_specs=[
            pl.BlockSpec(blk, lambda hp, g: (0, g, hp)),
            kspec(0, 1), kspec(1, 1), kspec(2, 1),
            kspec(0, 2), kspec(1, 2), kspec(2, 2),
            pl.BlockSpec((n_b, ctx, HP), lambda hp, g: (0, 0, n_hp + hp)),
            pl.BlockSpec((n_b, ctx, HP), lambda hp, g: (0, 0, 2 * n_hp + hp)),
            pl.BlockSpec((None, 2, 1, QB, 3 * QB), lambda hp, g: (layer, hp, var(g), 0, 0)),
        ],
        out_specs=pl.BlockSpec(blk, lambda hp, g: (0, g, hp)),
        compiler_params=_cparams(("arbitrary", "arbitrary"), 40),
        name="nbr_attn",
    )(qkv, qkv, qkv, qkv, qkv, qkv, qkv, qkv_ctx, qkv_ctx, bias)


def _ctx_attn_kernel(n_b, q_ref, k_ref, v_ref, o_ref):
    n_q = q_ref.shape[1]
    lo = lax.broadcasted_iota(jnp.int32, (n_q, HP), 1) < HEAD_DIM
    zero = jnp.zeros((n_q, HP), BF16)

    def body(b, carry):
        q = q_ref[b]
        outs = []
        for hh in range(2):
            qm = jnp.where(lo, q, zero) if hh == 0 else jnp.where(lo, zero, q)
            outs.append(_softmax_pv([_dot_nt(qm, k_ref[b])], [v_ref[b]]))
        o_ref[b] = jnp.where(lo, outs[0], outs[1]).astype(BF16)
        return carry

    lax.fori_loop(0, n_b, body, 0)


def _ctx_attn(qkv_ctx):
    n_b, ctx, _ = qkv_ctx.shape
    n_hp = ATTN_DIM // HP
    blk = (n_b, ctx, HP)
    return pl.pallas_call(
        functools.partial(_ctx_attn_kernel, n_b),
        out_shape=jax.ShapeDtypeStruct((n_b, ctx, ATTN_DIM), BF16),
        grid=(n_hp,),
        in_specs=[
            pl.BlockSpec(blk, lambda hp: (0, 0, hp)),
            pl.BlockSpec(blk, lambda hp: (0, 0, n_hp + hp)),
            pl.BlockSpec(blk, lambda hp: (0, 0, 2 * n_hp + hp)),
        ],
        out_specs=pl.BlockSpec(blk, lambda hp: (0, 0, hp)),
        compiler_params=_cparams(("arbitrary",), 32),
        name="ctx_attn",
    )(qkv_ctx, qkv_ctx, qkv_ctx)


def _attn_bias_index(rows):
    n_g = rows // QB_ROWS
    out_ro, out_co, out_ok = [], [], []
    qc = np.arange(GRID_W)
    col_start = np.clip(qc - WIN_COLS // 2, 0, GRID_W - WIN_COLS)
    kc = np.arange(GRID_W)
    col_ok = (kc[None, :] >= col_start[:, None]) & (kc[None, :] < col_start[:, None] + WIN_COLS)
    col_off = np.clip(kc[None, :] - qc[:, None], -(WIN_COLS - 1), WIN_COLS - 1) + WIN_COLS - 1
    for g in (0, 1, n_g - 1):
        r = g * QB_ROWS + np.arange(QB_ROWS)
        row_start = np.clip(r - WIN_ROWS // 2, 0, rows - WIN_ROWS)
        kr = (min(max(g - 1, 0), n_g - 3)) * QB_ROWS + np.arange(3 * QB_ROWS)
        row_ok = (kr[None, :] >= row_start[:, None]) & (kr[None, :] < row_start[:, None] + WIN_ROWS)
        row_off = np.clip(kr[None, :] - r[:, None] + WIN_ROWS - 1, 0, 2 * WIN_ROWS - 2)
        ro = np.broadcast_to(row_off[:, None, :, None], (QB_ROWS, GRID_W, 3 * QB_ROWS, GRID_W))
        co = np.broadcast_to(col_off[None, :, None, :], (QB_ROWS, GRID_W, 3 * QB_ROWS, GRID_W))
        ok = row_ok[:, None, :, None] & col_ok[None, :, None, :]
        out_ro.append(ro.reshape(QB, 3 * QB))
        out_co.append(co.reshape(QB, 3 * QB))
        out_ok.append(ok.reshape(QB, 3 * QB))
    return np.stack(out_ro), np.stack(out_co), np.stack(out_ok)


def _attn_bias(rpb, rows):
    ro, co, ok = _attn_bias_index(rows)
    b = rpb[:, :, ro, co]
    return jnp.where(ok[None, None], b, NEG_INF).astype(F32)


def _outproj_kernel(tiles_per_batch, ctx_row, conv_ref, gm_ref, at_ref, gate_ref, x_ref, mg_ref,
                    wc_ref, wg_ref, wa_ref, wo_ref, o_ref):
    i = pl.program_id(0)
    row = _mod_row(i, tiles_per_batch, ctx_row)
    y = gate_ref[:, 0:D_MODEL].astype(F32) * _dot(conv_ref[...], wc_ref[...])
    y = y + gate_ref[:, D_MODEL:2 * D_MODEL].astype(F32) * _dot(gm_ref[...], wg_ref[...])
    y = y + gate_ref[:, 2 * D_MODEL:3 * D_MODEL].astype(F32) * _dot(at_ref[...], wa_ref[...])
    o_ref[...] = x_ref[...] + mg_ref[pl.ds(row, 1), :] * _dot(y.astype(BF16), wo_ref[...])


def _outproj(conv, gm, attn, gates, s, mod, layer, wc, wg, wa, wo, *, seq, ctx_row):
    rows = s.shape[0]
    tm = TM_OUT
    tiles_per_batch = None if seq is None else seq // tm
    act = lambda n: pl.BlockSpec((tm, n), lambda i: (i, 0))
    wspec = lambda k: pl.BlockSpec((None, k, D_MODEL), lambda i: (layer, 0, 0),
                                   pipeline_mode=pl.Buffered(1))
    return pl.pallas_call(
        functools.partial(_outproj_kernel, tiles_per_batch, ctx_row),
        out_shape=jax.ShapeDtypeStruct((rows, D_MODEL), F32),
        grid=(rows // tm,),
        in_specs=[
            act(CONV_DIM), act(GMLP_DIM), act(ATTN_DIM), act(3 * D_MODEL), act(D_MODEL),
            pl.BlockSpec((None, MOD_ROWS, D_MODEL), lambda i: (layer, 0, 5)),
            wspec(CONV_DIM), wspec(GMLP_DIM), wspec(ATTN_DIM), wspec(D_MODEL),
        ],
        out_specs=act(D_MODEL),
        compiler_params=_cparams(("arbitrary",), 52),
        name="outproj",
    )(conv, gm, attn, gates, s, mod, wc, wg, wa, wo)


def kernel(x, c, ctx, c_ctx, w_mod, b_mod, norm_ffn1, ffn1_w_gu, ffn1_w_down, norm_mix, w_in, conv_dw, conv_db, conv_ln_g, conv_ln_b, gmlp_ln_g, gmlp_ln_b, gmlp_ws, gmlp_bs, attn_rpb, w_conv_out, w_gmlp_out, w_attn_out, w_out, norm_ffn2, ffn2_w_gu, ffn2_w_down, final_norm):
    n_b, seq, d = x.shape
    n_ctx = ctx.shape[1]
    depth = w_mod.shape[0]
    assert d == D_MODEL and n_b < MOD_ROWS and seq % QB == 0 and seq // QB >= 3
    assert seq % TM_FFN == 0 and (n_b * n_ctx) % TM_FFN == 0 and n_ctx % TQ_CONV == 0
    ctx_row = n_b

    cc = jnp.concatenate([c, c_ctx[None, :], jnp.zeros((MOD_ROWS - n_b - 1, d), F32)], axis=0)
    mod = _mod_table(cc, w_mod, b_mod)

    vec3 = lambda a: a.reshape(depth, 1, a.shape[-1])
    norm_ffn1, norm_mix, norm_ffn2 = vec3(norm_ffn1), vec3(norm_mix), vec3(norm_ffn2)
    conv_db, conv_ln_g, conv_ln_b = vec3(conv_db), vec3(conv_ln_g), vec3(conv_ln_b)
    gmlp_ln_g, gmlp_ln_b = vec3(gmlp_ln_g), vec3(gmlp_ln_b)
    bst = jnp.swapaxes(gmlp_bs, 1, 2)
    ws = gmlp_ws.astype(BF16)
    w1gu, w1d = ffn1_w_gu.astype(BF16), ffn1_w_down.astype(BF16)
    w2gu, w2d = ffn2_w_gu.astype(BF16), ffn2_w_down.astype(BF16)
    w_in_b = w_in.astype(BF16)
    wc, wg, wa, wo = (w.astype(BF16) for w in (w_conv_out, w_gmlp_out, w_attn_out, w_out))
    bias = _attn_bias(attn_rpb, seq // GRID_W)
    final_g = final_norm.reshape(1, d)

    xs = x.reshape(n_b * seq, d)
    cs = ctx.reshape(n_b * n_ctx, d)
    lat = dict(seq=seq, ctx_row=ctx_row)
    cxt = dict(seq=None, ctx_row=ctx_row)

    for i in range(depth):
        last = i == depth - 1
        xs = _ffn(xs, mod, i, 0, norm_ffn1, w1gu, w1d, **lat)
        cs = _ffn(cs, mod, i, 0, norm_ffn1, w1gu, w1d, **cxt)
        a_x, gm_x, qkv_x, gate_x = _inproj(xs, mod, i, norm_mix, w_in_b, gmlp_ln_g, gmlp_ln_b, ws, bst, **lat)
        a_c, gm_c, qkv_c, gate_c = _inproj(cs, mod, i, norm_mix, w_in_b, gmlp_ln_g, gmlp_ln_b, ws, bst, **cxt)
        qkv_c3 = qkv_c.reshape(n_b, n_ctx, 3 * ATTN_DIM)
        attn_x = _nbr_attn(qkv_x.reshape(n_b, seq, 3 * ATTN_DIM), qkv_c3, bias, i)
        conv_x = _conv(a_x, i, conv_dw, conv_db, conv_ln_g, conv_ln_b, seq=seq)
        xs = _outproj(conv_x, gm_x, attn_x.reshape(n_b * seq, ATTN_DIM), gate_x, xs, mod, i,
                      wc, wg, wa, wo, **lat)
        xs = _ffn(xs, mod, i, 2, norm_ffn2, w2gu, w2d, final_g=final_g if last else None, **lat)
        if not last:
            attn_c = _ctx_attn(qkv_c3)
            conv_c = _conv(a_c, i, conv_dw, conv_db, conv_ln_g, conv_ln_b, seq=n_ctx)
            cs = _outproj(conv_c, gm_c, attn_c.reshape(n_b * n_ctx, ATTN_DIM), gate_c, cs, mod, i,
                          wc, wg, wa, wo, **cxt)
            cs = _ffn(cs, mod, i, 2, norm_ffn2, w2gu, w2d, **cxt)
    return xs.reshape(n_b, seq, d)
```

```python
import functools
import math

import numpy as np
import jax
import jax.numpy as jnp
from jax import lax
from jax.experimental import pallas as pl
from jax.experimental.pallas import tpu as pltpu

D_MODEL = 2048
D_FF = 5632
CONV_DIM = 1024
CONV_WIDTH = 31
GMLP_DIM = 1024
GMLP_CHUNK = 128
GMLP_GROUPS = 4
N_HEADS = 16
HEAD_DIM = 64
ATTN_DIM = N_HEADS * HEAD_DIM
GRID_W = 64
WIN_ROWS = 8
WIN_COLS = 16
N_MOD = 9
EPS = 1e-6
NEG_INF = -1e9

O_CONV = 0
O_GMLP = O_CONV + 2 * CONV_DIM
O_Q = O_GMLP + 2 * GMLP_DIM
O_GATE = O_Q + 3 * ATTN_DIM
IN_COLS = O_GATE + 3 * D_MODEL

MOD_ROWS = 8
V7X_LANES = 128
V7X_VMEM_BYTES = 64 * 1024 * 1024

F32 = jnp.float32
BF16 = jnp.bfloat16

TM_FFN = 1024
TF_FFN = 512
ROW_CHUNK = 64
TM_IN = 512
TN_IN = 1024
TM_OUT = 256
TQ_CONV = 256
HALO = 16
QB_ROWS = 4
QB = QB_ROWS * GRID_W
HP = 2 * HEAD_DIM


def _sigmoid(x):
    return 1.0 / (1.0 + jnp.exp(-x))


def _dot(a, b):
    return jnp.dot(a, b, preferred_element_type=F32)


def _dot_nt(a, b):
    return lax.dot_general(a, b, (((1,), (1,)), ((), ())), preferred_element_type=F32)


def _rms_mod(x, g, shift, scale):
    y = x * lax.rsqrt(jnp.mean(x * x, axis=-1, keepdims=True) + EPS)
    return (y * g) * (1.0 + scale) + shift


def _layer_norm(x, g, b):
    mu = jnp.mean(x, axis=-1, keepdims=True)
    xc = x - mu
    var = jnp.mean(xc * xc, axis=-1, keepdims=True)
    return xc * lax.rsqrt(var + EPS) * g + b


def _mod_row(i, tiles_per_batch, ctx_row):
    if tiles_per_batch is None:
        return ctx_row
    return i // tiles_per_batch


def _cparams(sem, vmem_mb):
    return pltpu.CompilerParams(dimension_semantics=sem, vmem_limit_bytes=vmem_mb * 1024 * 1024)


def _mod_kernel(c_ref, w_ref, b_ref, o_ref):
    c = c_ref[...]
    s = (c * _sigmoid(c)).astype(BF16)
    o_ref[...] = _dot(s, w_ref[...].astype(BF16)) + b_ref[...]


def _mod_table(cc, w_mod, b_mod):
    depth = w_mod.shape[0]
    tn = 1024
    n_cols = N_MOD * D_MODEL
    return pl.pallas_call(
        _mod_kernel,
        out_shape=jax.ShapeDtypeStruct((depth, MOD_ROWS, n_cols), F32),
        grid=(depth, n_cols // tn),
        in_specs=[
            pl.BlockSpec((MOD_ROWS, D_MODEL), lambda l, j: (0, 0)),
            pl.BlockSpec((None, D_MODEL, tn), lambda l, j: (l, 0, j)),
            pl.BlockSpec((None, 1, tn), lambda l, j: (l, 0, j)),
        ],
        out_specs=pl.BlockSpec((None, MOD_ROWS, tn), lambda l, j: (l, 0, j)),
        compiler_params=_cparams(("arbitrary", "arbitrary"), 40),
        name="mod_table",
    )(cc, w_mod, b_mod.reshape(depth, 1, n_cols))


def _ffn_kernel(tiles_per_batch, ctx_row, n_f, final, *refs):
    if final:
        (x_ref, sh_ref, sc_ref, gt_ref, g_ref, wg_ref, wu_ref, wd_ref, fg_ref,
         o_ref, h_ref) = refs
    else:
        (x_ref, sh_ref, sc_ref, gt_ref, g_ref, wg_ref, wu_ref, wd_ref,
         o_ref, h_ref) = refs
    i = pl.program_id(0)
    f = pl.program_id(1)
    row = _mod_row(i, tiles_per_batch, ctx_row)

    n_chunks = x_ref.shape[0] // ROW_CHUNK

    @pl.when(f == 0)
    def _():
        g = g_ref[...]
        shift, scale = sh_ref[pl.ds(row, 1), :], sc_ref[pl.ds(row, 1), :]

        def chunk(r, carry):
            rows = pl.ds(pl.multiple_of(r * ROW_CHUNK, ROW_CHUNK), ROW_CHUNK)
            h_ref[rows, :] = _rms_mod(x_ref[rows, :], g, shift, scale).astype(BF16)
            o_ref[rows, :] = jnp.zeros((ROW_CHUNK, D_MODEL), F32)
            return carry

        lax.fori_loop(0, n_chunks, chunk, 0)

    h = h_ref[...]
    g = _dot(h, wg_ref[...])
    u = _dot(h, wu_ref[...])
    a = ((g * _sigmoid(g)) * u).astype(BF16)
    o_ref[...] += _dot(a, wd_ref[...])

    @pl.when(f == n_f - 1)
    def _():
        gate = 0.5 * gt_ref[pl.ds(row, 1), :]

        def chunk(r, carry):
            rows = pl.ds(pl.multiple_of(r * ROW_CHUNK, ROW_CHUNK), ROW_CHUNK)
            y = x_ref[rows, :] + gate * o_ref[rows, :]
            if final:
                y = y * lax.rsqrt(jnp.mean(y * y, axis=-1, keepdims=True) + EPS) * fg_ref[...]
            o_ref[rows, :] = y
            return carry

        lax.fori_loop(0, n_chunks, chunk, 0)


def _ffn(s, mod, layer, j, norm_g, w_gu, w_down, *, seq, ctx_row, final_g=None):
    rows = s.shape[0]
    tm, tf = min(TM_FFN, rows), TF_FFN
    n_f = D_FF // tf
    tiles_per_batch = None if seq is None else seq // tm
    final = final_g is not None
    mod_spec = lambda k: pl.BlockSpec((None, MOD_ROWS, D_MODEL), lambda i, f: (layer, 0, 3 * j + k))
    in_specs = [
        pl.BlockSpec((tm, D_MODEL), lambda i, f: (i, 0)),
        mod_spec(0), mod_spec(1), mod_spec(2),
        pl.BlockSpec((None, 1, D_MODEL), lambda i, f: (layer, 0, 0)),
        pl.BlockSpec((None, D_MODEL, tf), lambda i, f: (layer, 0, f)),
        pl.BlockSpec((None, D_MODEL, tf), lambda i, f: (layer, 0, n_f + f)),
        pl.BlockSpec((None, tf, D_MODEL), lambda i, f: (layer, f, 0)),
    ]
    args = [s, mod, mod, mod, norm_g, w_gu, w_gu, w_down]
    if final:
        in_specs.append(pl.BlockSpec((1, D_MODEL), lambda i, f: (0, 0)))
        args.append(final_g)
    return pl.pallas_call(
        functools.partial(_ffn_kernel, tiles_per_batch, ctx_row, n_f, final),
        out_shape=jax.ShapeDtypeStruct((rows, D_MODEL), F32),
        grid=(rows // tm, n_f),
        in_specs=in_specs,
        out_specs=pl.BlockSpec((tm, D_MODEL), lambda i, f: (i, 0)),
        scratch_shapes=[pltpu.VMEM((tm, D_MODEL), BF16)],
        compiler_params=_cparams(("arbitrary", "arbitrary"), 56),
        name="ffn",
    )(*args)


def _gelu_tanh(x):
    c = math.sqrt(2.0 / math.pi)
    return x * (0.5 * (1.0 + jnp.tanh(c * (x + 0.044715 * (x * x * x)))))


def _inproj_kernel(tiles_per_batch, ctx_row, x_ref, sh_ref, sc_ref, g_ref, w_ref, lng_ref, lnb_ref,
                   ws_ref, bst_ref, a_ref, gm_ref, qkv_ref, gate_ref, h_ref, pa_ref, u_ref):
    i = pl.program_id(0)
    j = pl.program_id(1)
    row = _mod_row(i, tiles_per_batch, ctx_row)
    tm = x_ref.shape[0]

    @pl.when(j == 0)
    def _():
        g = g_ref[...]
        shift, scale = sh_ref[pl.ds(row, 1), :], sc_ref[pl.ds(row, 1), :]

        def chunk(r, carry):
            rows = pl.ds(pl.multiple_of(r * ROW_CHUNK, ROW_CHUNK), ROW_CHUNK)
            h_ref[rows, :] = _rms_mod(x_ref[rows, :], g, shift, scale).astype(BF16)
            return carry

        lax.fori_loop(0, tm // ROW_CHUNK, chunk, 0)

    def proj():
        return _dot(h_ref[...], w_ref[...])

    @pl.when(j == 0)
    def _():
        pa_ref[...] = proj()

    @pl.when(j == 1)
    def _():
        a_ref[...] = (pa_ref[...] * _sigmoid(proj())).astype(BF16)

    @pl.when(j == 2)
    def _():
        u_ref[...] = _gelu_tanh(proj())

    @pl.when(j == 3)
    def _():
        vn = _layer_norm(_gelu_tanh(proj()), lng_ref[...], lnb_ref[...]).astype(BF16)
        gw = GMLP_DIM // GMLP_GROUPS
        for c in range(tm // GMLP_CHUNK):
            r0 = c * GMLP_CHUNK
            for g in range(GMLP_GROUPS):
                sp = _dot(ws_ref[g], vn[r0:r0 + GMLP_CHUNK, g * gw:(g + 1) * gw]) + bst_ref[:, g:g + 1]
                gm_ref[r0:r0 + GMLP_CHUNK, g * gw:(g + 1) * gw] = (
                    u_ref[r0:r0 + GMLP_CHUNK, g * gw:(g + 1) * gw] * sp).astype(BF16)

    @pl.when(j == 4)
    def _():
        qkv_ref[...] = (proj() * (HEAD_DIM ** -0.5)).astype(BF16)

    @pl.when((j == 5) | (j == 6))
    def _():
        qkv_ref[...] = proj().astype(BF16)

    @pl.when(j >= 7)
    def _():
        gate_ref[...] = _sigmoid(proj()).astype(BF16)


def _inproj(s, mod, layer, norm_g, w_in, ln_g, ln_b, ws, bst, *, seq, ctx_row):
    rows = s.shape[0]
    tm, tn = TM_IN, TN_IN
    n_j = IN_COLS // tn
    j_q, j_gate = O_Q // tn, O_GATE // tn
    tiles_per_batch = None if seq is None else seq // tm
    mod_spec = lambda k: pl.BlockSpec((None, MOD_ROWS, D_MODEL), lambda i, j: (layer, 0, 3 + k))
    vec = lambda n: pl.BlockSpec((None, 1, n), lambda i, j: (layer, 0, 0))
    return pl.pallas_call(
        functools.partial(_inproj_kernel, tiles_per_batch, ctx_row),
        out_shape=(
            jax.ShapeDtypeStruct((rows, CONV_DIM), BF16),
            jax.ShapeDtypeStruct((rows, GMLP_DIM), BF16),
            jax.ShapeDtypeStruct((rows, 3 * ATTN_DIM), BF16),
            jax.ShapeDtypeStruct((rows, 3 * D_MODEL), BF16),
        ),
        grid=(rows // tm, n_j),
        in_specs=[
            pl.BlockSpec((tm, D_MODEL), lambda i, j: (i, 0)),
            mod_spec(0), mod_spec(1),
            vec(D_MODEL),
            pl.BlockSpec((None, D_MODEL, tn), lambda i, j: (layer, 0, j)),
            vec(GMLP_DIM), vec(GMLP_DIM),
            pl.BlockSpec((None, GMLP_GROUPS, GMLP_CHUNK, GMLP_CHUNK), lambda i, j: (layer, 0, 0, 0)),
            pl.BlockSpec((None, GMLP_CHUNK, GMLP_GROUPS), lambda i, j: (layer, 0, 0)),
        ],
        out_specs=(
            pl.BlockSpec((tm, CONV_DIM), lambda i, j: (i, 0)),
            pl.BlockSpec((tm, GMLP_DIM), lambda i, j: (i, 0)),
            pl.BlockSpec((tm, tn), lambda i, j: (i, jnp.clip(j - j_q, 0, 2))),
            pl.BlockSpec((tm, tn), lambda i, j: (i, jnp.clip(j - j_gate, 0, n_j - j_gate - 1))),
        ),
        scratch_shapes=[pltpu.VMEM((tm, D_MODEL), BF16), pltpu.VMEM((tm, CONV_DIM), F32),
                        pltpu.VMEM((tm, GMLP_DIM), F32)],
        compiler_params=_cparams(("arbitrary", "arbitrary"), 48),
        name="inproj",
    )(s, mod, mod, norm_g, w_in, ln_g, ln_b, ws, bst)


def _conv_kernel(tiles_per_seq, prev_ref, cur_ref, next_ref, dw_ref, db_ref, lng_ref, lnb_ref,
                 o_ref, win_ref, sh_ref, acc_ref):
    tq = cur_ref.shape[0]
    pos = pl.program_id(0) % tiles_per_seq
    win_ref[0:HALO, :] = jnp.where(pos > 0, prev_ref[...].astype(F32), 0.0)
    win_ref[HALO:HALO + tq, :] = cur_ref[...].astype(F32)
    win_ref[HALO + tq:HALO + tq + HALO, :] = jnp.where(pos < tiles_per_seq - 1, next_ref[...].astype(F32), 0.0)
    n_sh = sh_ref.shape[1]
    for r in range(8):
        sh_ref[r] = win_ref[r:r + n_sh, :]
    rb = 64
    first = HALO - CONV_WIDTH // 2

    def body(t, carry):
        base = pl.multiple_of(t * rb, rb)
        for lb in range(CONV_DIM // V7X_LANES):
            lanes = slice(lb * V7X_LANES, (lb + 1) * V7X_LANES)
            acc = jnp.zeros((rb, V7X_LANES), F32)
            for k in range(CONV_WIDTH):
                off = first + k
                acc = acc + sh_ref[off % 8, pl.ds(base + (off // 8) * 8, rb), lanes] * dw_ref[k:k + 1, lanes]
            acc_ref[pl.ds(base, rb), lanes] = acc
        return carry

    lax.fori_loop(0, tq // rb, body, 0)
    y = _layer_norm(acc_ref[...] + db_ref[...], lng_ref[...], lnb_ref[...])
    o_ref[...] = (y * _sigmoid(y)).astype(BF16)


def _conv(a, layer, dw, db, ln_g, ln_b, *, seq):
    rows = a.shape[0]
    tq = TQ_CONV
    tiles_per_seq = seq // tq
    halo_blocks = rows // HALO
    per = tq // HALO
    vec = pl.BlockSpec((None, 1, CONV_DIM), lambda t: (layer, 0, 0))
    n_sh = tq + 2 * HALO - 8
    return pl.pallas_call(
        functools.partial(_conv_kernel, tiles_per_seq),
        out_shape=jax.ShapeDtypeStruct((rows, CONV_DIM), BF16),
        grid=(rows // tq,),
        in_specs=[
            pl.BlockSpec((HALO, CONV_DIM), lambda t: (jnp.maximum(t * per - 1, 0), 0)),
            pl.BlockSpec((tq, CONV_DIM), lambda t: (t, 0)),
            pl.BlockSpec((HALO, CONV_DIM), lambda t: (jnp.minimum((t + 1) * per, halo_blocks - 1), 0)),
            pl.BlockSpec((None, CONV_WIDTH, CONV_DIM), lambda t: (layer, 0, 0)),
            vec, vec, vec,
        ],
        out_specs=pl.BlockSpec((tq, CONV_DIM), lambda t: (t, 0)),
        scratch_shapes=[pltpu.VMEM((tq + 2 * HALO, CONV_DIM), F32),
                        pltpu.VMEM((8, n_sh, CONV_DIM), F32),
                        pltpu.VMEM((tq, CONV_DIM), F32)],
        compiler_params=_cparams(("arbitrary",), 40),
        name="conv",
    )(a, a, a, dw, db, ln_g, ln_b)


def _softmax_pv(s, v):
    p = jnp.exp(s - s.max(axis=-1, keepdims=True))
    return _dot(p.astype(BF16), v) / p.sum(axis=-1, keepdims=True)


def _nbr_attn_kernel(n_b, n_g, plan_ref, q_ref, k0_ref, k1_ref, k2_ref, v0_ref, v1_ref, v2_ref,
                     kc_ref, vc_ref, tab_ref, o_ref, bias_ref):
    lo = lax.broadcasted_iota(jnp.int32, (QB, HP), 1) < HEAD_DIM
    zero = jnp.zeros((QB, HP), BF16)

    g = pl.program_id(1)
    kind = jnp.where(g == 0, 0, jnp.where(g == n_g - 1, 2, 1))
    pairs = QB_ROWS // 2
    per_kind = 3 * pairs * QB_ROWS
    for hh in range(2):
        for t in range(3):
            for dr in range(QB_ROWS):
                for p in range(pairs):
                    e = plan_ref[kind * per_kind + (t * QB_ROWS + dr) * pairs + p]
                    c0 = t * QB + p * 2 * GRID_W
                    bias_ref[hh, 0, dr * GRID_W:(dr + 1) * GRID_W, c0:c0 + 2 * GRID_W] = tab_ref[hh, e]

    def body(b, carry):
        q = q_ref[b]
        k = jnp.concatenate([k0_ref[b], k1_ref[b], k2_ref[b], kc_ref[b]], axis=0)
        v = jnp.concatenate([v0_ref[b], v1_ref[b], v2_ref[b], vc_ref[b]], axis=0)
        outs = []
        for hh in range(2):
            qm = jnp.where(lo, q, zero) if hh == 0 else jnp.where(lo, zero, q)
            s = _dot_nt(qm, k)
            s = jnp.concatenate([s[:, :3 * QB] + bias_ref[hh, 0], s[:, 3 * QB:]], axis=1)
            outs.append(_softmax_pv(s, v))
        o_ref[b] = jnp.where(lo, outs[0], outs[1]).astype(BF16)
        return carry

    lax.fori_loop(0, n_b, body, 0, unroll=True)


def _nbr_attn(qkv, qkv_ctx, table, plan):
    n_b, seq, _ = qkv.shape
    ctx = qkv_ctx.shape[1]
    n_g = seq // QB
    n_hp = ATTN_DIM // HP
    n_e = table.shape[1]
    kb = lambda g: jnp.clip(g - 1, 0, n_g - 3)
    blk = (n_b, QB, HP)
    kspec = lambda t, col: pl.BlockSpec(blk, lambda hp, g, plan: (0, kb(g) + t, col * n_hp + hp))
    return pl.pallas_call(
        functools.partial(_nbr_attn_kernel, n_b, n_g),
        out_shape=jax.ShapeDtypeStruct((n_b, seq, ATTN_DIM), BF16),
        grid_spec=pltpu.PrefetchScalarGridSpec(
            num_scalar_prefetch=1,
            grid=(n_hp, n_g),
            in_specs=[
                pl.BlockSpec(blk, lambda hp, g, plan: (0, g, hp)),
                kspec(0, 1), kspec(1, 1), kspec(2, 1),
                kspec(0, 2), kspec(1, 2), kspec(2, 2),
                pl.BlockSpec((n_b, ctx, HP), lambda hp, g, plan: (0, 0, n_hp + hp)),
                pl.BlockSpec((n_b, ctx, HP), lambda hp, g, plan: (0, 0, 2 * n_hp + hp)),
                pl.BlockSpec((2, n_e, GRID_W, 2 * GRID_W), lambda hp, g, plan: (hp, 0, 0, 0)),
            ],
            out_specs=pl.BlockSpec(blk, lambda hp, g, plan: (0, g, hp)),
            scratch_shapes=[pltpu.VMEM((2, 1, QB, 3 * QB), F32)],
        ),
        compiler_params=_cparams(("arbitrary", "arbitrary"), 40),
        name="nbr_attn",
    )(plan, qkv, qkv, qkv, qkv, qkv, qkv, qkv, qkv_ctx, qkv_ctx, table)


def _ctx_attn_kernel(n_b, q_ref, k_ref, v_ref, o_ref):
    n_q = q_ref.shape[1]
    lo = lax.broadcasted_iota(jnp.int32, (n_q, HP), 1) < HEAD_DIM
    zero = jnp.zeros((n_q, HP), BF16)

    def body(b, carry):
        q = q_ref[b]
        outs = []
        for hh in range(2):
            qm = jnp.where(lo, q, zero) if hh == 0 else jnp.where(lo, zero, q)
            outs.append(_softmax_pv(_dot_nt(qm, k_ref[b]), v_ref[b]))
        o_ref[b] = jnp.where(lo, outs[0], outs[1]).astype(BF16)
        return carry

    lax.fori_loop(0, n_b, body, 0)


def _ctx_attn(qkv_ctx):
    n_b, ctx, _ = qkv_ctx.shape
    n_hp = ATTN_DIM // HP
    blk = (n_b, ctx, HP)
    return pl.pallas_call(
        functools.partial(_ctx_attn_kernel, n_b),
        out_shape=jax.ShapeDtypeStruct((n_b, ctx, ATTN_DIM), BF16),
        grid=(n_hp,),
        in_specs=[
            pl.BlockSpec(blk, lambda hp: (0, 0, hp)),
            pl.BlockSpec(blk, lambda hp: (0, 0, n_hp + hp)),
            pl.BlockSpec(blk, lambda hp: (0, 0, 2 * n_hp + hp)),
        ],
        out_specs=pl.BlockSpec(blk, lambda hp: (0, 0, hp)),
        compiler_params=_cparams(("arbitrary",), 32),
        name="ctx_attn",
    )(qkv_ctx, qkv_ctx, qkv_ctx)


def _attn_bias_plan(rows):
    n_g = rows // QB_ROWS
    masked = 2 * WIN_ROWS - 1
    pairs = {}
    plan = np.zeros((3, 3, QB_ROWS, QB_ROWS // 2), np.int32)
    for kind, g in enumerate((0, 1, n_g - 1)):
        r = g * QB_ROWS + np.arange(QB_ROWS)
        row_start = np.clip(r - WIN_ROWS // 2, 0, rows - WIN_ROWS)
        kr = min(max(g - 1, 0), n_g - 3) * QB_ROWS + np.arange(3 * QB_ROWS)
        row_ok = (kr[None, :] >= row_start[:, None]) & (kr[None, :] < row_start[:, None] + WIN_ROWS)
        row_off = kr[None, :] - r[:, None] + WIN_ROWS - 1
        assert np.all((row_off[row_ok] >= 0) & (row_off[row_ok] < masked))
        ro = np.where(row_ok, row_off, masked)
        for t in range(3):
            for dr in range(QB_ROWS):
                for p in range(QB_ROWS // 2):
                    k0 = t * QB_ROWS + 2 * p
                    key = (int(ro[dr, k0]), int(ro[dr, k0 + 1]))
                    plan[kind, t, dr, p] = pairs.setdefault(key, len(pairs))
    return tuple(pairs), plan.reshape(-1)


def _bias_table_kernel(pairs, rpb_ref, o_ref, rows_ref):
    h = pl.program_id(0)
    n_ro, n_co = 2 * WIN_ROWS - 1, 2 * WIN_COLS - 1
    shape = (GRID_W, 2 * GRID_W)
    qc = lax.broadcasted_iota(jnp.int32, shape, 0)
    lane = lax.broadcasted_iota(jnp.int32, shape, 1)
    kc = lane & (GRID_W - 1)
    col_start = jnp.clip(qc - WIN_COLS // 2, 0, GRID_W - WIN_COLS)
    off = jnp.clip(kc - qc, -(WIN_COLS - 1), WIN_COLS - 1) + WIN_COLS - 1
    off = jnp.where(kc >= col_start, jnp.where(kc < col_start + WIN_COLS, off, -1), -1)
    neg = jnp.full(shape, NEG_INF, F32)
    for ro in range(n_ro):
        val = neg
        for j in range(n_co):
            val = jnp.where(off == j, rpb_ref[(h * n_ro + ro) * n_co + j], val)
        rows_ref[ro] = val
    rows_ref[n_ro] = neg
    for e, (a, b) in enumerate(pairs):
        o_ref[e] = jnp.where(lane < GRID_W, rows_ref[a], rows_ref[b])


def _bias_table(rpb_flat, pairs):
    n_e = len(pairs)
    return pl.pallas_call(
        functools.partial(_bias_table_kernel, pairs),
        out_shape=jax.ShapeDtypeStruct((N_HEADS, n_e, GRID_W, 2 * GRID_W), F32),
        grid=(N_HEADS,),
        in_specs=[pl.BlockSpec(memory_space=pltpu.SMEM)],
        out_specs=pl.BlockSpec((None, n_e, GRID_W, 2 * GRID_W), lambda h: (h, 0, 0, 0)),
        scratch_shapes=[pltpu.VMEM((2 * WIN_ROWS, GRID_W, 2 * GRID_W), F32)],
        compiler_params=_cparams(("arbitrary",), 16),
        name="bias_table",
    )(rpb_flat)


def _outproj_kernel(tiles_per_batch, ctx_row, conv_ref, gm_ref, at_ref, gate_ref, x_ref, mg_ref,
                    wc_ref, wg_ref, wa_ref, wo_ref, o_ref):
    i = pl.program_id(0)
    row = _mod_row(i, tiles_per_batch, ctx_row)
    y = gate_ref[:, 0:D_MODEL].astype(F32) * _dot(conv_ref[...], wc_ref[...])
    y = y + gate_ref[:, D_MODEL:2 * D_MODEL].astype(F32) * _dot(gm_ref[...], wg_ref[...])
    y = y + gate_ref[:, 2 * D_MODEL:3 * D_MODEL].astype(F32) * _dot(at_ref[...], wa_ref[...])
    o_ref[...] = x_ref[...] + mg_ref[pl.ds(row, 1), :] * _dot(y.astype(BF16), wo_ref[...])


def _outproj(conv, gm, attn, gates, s, mod, layer, wc, wg, wa, wo, *, seq, ctx_row):
    rows = s.shape[0]
    tm = TM_OUT
    tiles_per_batch = None if seq is None else seq // tm
    act = lambda n: pl.BlockSpec((tm, n), lambda i: (i, 0))
    wspec = lambda k: pl.BlockSpec((None, k, D_MODEL), lambda i: (layer, 0, 0),
                                   pipeline_mode=pl.Buffered(1))
    return pl.pallas_call(
        functools.partial(_outproj_kernel, tiles_per_batch, ctx_row),
        out_shape=jax.ShapeDtypeStruct((rows, D_MODEL), F32),
        grid=(rows // tm,),
        in_specs=[
            act(CONV_DIM), act(GMLP_DIM), act(ATTN_DIM), act(3 * D_MODEL), act(D_MODEL),
            pl.BlockSpec((None, MOD_ROWS, D_MODEL), lambda i: (layer, 0, 5)),
            wspec(CONV_DIM), wspec(GMLP_DIM), wspec(ATTN_DIM), wspec(D_MODEL),
        ],
        out_specs=act(D_MODEL),
        compiler_params=_cparams(("arbitrary",), 52),
        name="outproj",
    )(conv, gm, attn, gates, s, mod, wc, wg, wa, wo)


def kernel(x, c, ctx, c_ctx, w_mod, b_mod, norm_ffn1, ffn1_w_gu, ffn1_w_down, norm_mix, w_in, conv_dw, conv_db, conv_ln_g, conv_ln_b, gmlp_ln_g, gmlp_ln_b, gmlp_ws, gmlp_bs, attn_rpb, w_conv_out, w_gmlp_out, w_attn_out, w_out, norm_ffn2, ffn2_w_gu, ffn2_w_down, final_norm):
    n_b, seq, d = x.shape
    n_ctx = ctx.shape[1]
    depth = w_mod.shape[0]
    assert d == D_MODEL and n_b < MOD_ROWS and seq % QB == 0 and seq // QB >= 3
    assert seq % TM_FFN == 0 and (n_b * n_ctx) % TM_IN == 0 and n_ctx % TQ_CONV == 0
    ctx_row = n_b

    cc = jnp.concatenate([c, c_ctx[None, :], jnp.zeros((MOD_ROWS - n_b - 1, d), F32)], axis=0)
    mod = _mod_table(cc, w_mod, b_mod)

    vec3 = lambda a: a.reshape(depth, 1, a.shape[-1])
    norm_ffn1, norm_mix, norm_ffn2 = vec3(norm_ffn1), vec3(norm_mix), vec3(norm_ffn2)
    conv_db, conv_ln_g, conv_ln_b = vec3(conv_db), vec3(conv_ln_g), vec3(conv_ln_b)
    gmlp_ln_g, gmlp_ln_b = vec3(gmlp_ln_g), vec3(gmlp_ln_b)
    bst = jnp.swapaxes(gmlp_bs, 1, 2)
    ws = gmlp_ws.astype(BF16)
    w1gu, w1d = ffn1_w_gu.astype(BF16), ffn1_w_down.astype(BF16)
    w2gu, w2d = ffn2_w_gu.astype(BF16), ffn2_w_down.astype(BF16)
    w_in_b = w_in.astype(BF16)
    wc, wg, wa, wo = (w.astype(BF16) for w in (w_conv_out, w_gmlp_out, w_attn_out, w_out))
    assert 2 * GRID_W == V7X_LANES and QB_ROWS % 2 == 0
    pairs, plan = _attn_bias_plan(seq // GRID_W)
    plan = jnp.asarray(plan)
    rpb_flat = attn_rpb.reshape(depth, -1)
    final_g = final_norm.reshape(1, d)

    xs = x.reshape(n_b * seq, d)
    cs = ctx.reshape(n_b * n_ctx, d)
    lat = dict(seq=seq, ctx_row=ctx_row)
    cxt = dict(seq=None, ctx_row=ctx_row)

    for i in range(depth):
        last = i == depth - 1
        xs = _ffn(xs, mod, i, 0, norm_ffn1, w1gu, w1d, **lat)
        cs = _ffn(cs, mod, i, 0, norm_ffn1, w1gu, w1d, **cxt)
        a_x, gm_x, qkv_x, gate_x = _inproj(xs, mod, i, norm_mix, w_in_b, gmlp_ln_g, gmlp_ln_b, ws, bst, **lat)
        a_c, gm_c, qkv_c, gate_c = _inproj(cs, mod, i, norm_mix, w_in_b, gmlp_ln_g, gmlp_ln_b, ws, bst, **cxt)
        qkv_c3 = qkv_c.reshape(n_b, n_ctx, 3 * ATTN_DIM)
        table = _bias_table(rpb_flat[i], pairs)
        attn_x = _nbr_attn(qkv_x.reshape(n_b, seq, 3 * ATTN_DIM), qkv_c3, table, plan)
        conv_x = _conv(a_x, i, conv_dw, conv_db, conv_ln_g, conv_ln_b, seq=seq)
        xs = _outproj(conv_x, gm_x, attn_x.reshape(n_b * seq, ATTN_DIM), gate_x, xs, mod, i,
                      wc, wg, wa, wo, **lat)
        xs = _ffn(xs, mod, i, 2, norm_ffn2, w2gu, w2d, final_g=final_g if last else None, **lat)
        if not last:
            attn_c = _ctx_attn(qkv_c3)
            conv_c = _conv(a_c, i, conv_dw, conv_db, conv_ln_g, conv_ln_b, seq=n_ctx)
            cs = _outproj(conv_c, gm_c, attn_c.reshape(n_b * n_ctx, ATTN_DIM), gate_c, cs, mod, i,
                          wc, wg, wa, wo, **cxt)
            cs = _ffn(cs, mod, i, 2, norm_ffn2, w2gu, w2d, **cxt)
    return xs.reshape(n_b, seq, d)
```

```python
import functools
import math

import numpy as np
import jax
import jax.numpy as jnp
from jax import lax
from jax.experimental import pallas as pl
from jax.experimental.pallas import tpu as pltpu

D_MODEL = 2048
D_FF = 5632
CONV_DIM = 1024
CONV_WIDTH = 31
GMLP_DIM = 1024
GMLP_CHUNK = 128
GMLP_GROUPS = 4
N_HEADS = 16
HEAD_DIM = 64
ATTN_DIM = N_HEADS * HEAD_DIM
GRID_W = 64
WIN_ROWS = 8
WIN_COLS = 16
N_MOD = 9
EPS = 1e-6
NEG_INF = -1e9

O_CONV = 0
O_GMLP = O_CONV + 2 * CONV_DIM
O_Q = O_GMLP + 2 * GMLP_DIM
O_GATE = O_Q + 3 * ATTN_DIM
IN_COLS = O_GATE + 3 * D_MODEL

MOD_ROWS = 8
V7X_LANES = 128
V7X_VMEM_BYTES = 64 * 1024 * 1024

F32 = jnp.float32
BF16 = jnp.bfloat16

TM_FFN = 1024
TF_FFN = 512
FFN_ROW_CHUNK = 256
TM_IN = 512
TN_IN = 1024
TM_OUT = 256
TQ_CONV = 256
HALO = 16
CONV_ROW_BLOCK = 64
QB_ROWS = 4
QB = QB_ROWS * GRID_W
HP = 2 * HEAD_DIM


def _sigmoid(x):
    return 1.0 / (1.0 + jnp.exp(-x))


def _dot(a, b):
    return jnp.dot(a, b, preferred_element_type=F32)


def _dot_nt(a, b):
    return lax.dot_general(a, b, (((1,), (1,)), ((), ())), preferred_element_type=F32)


def _rms_mod(x, g, shift, scale):
    y = x * lax.rsqrt(jnp.mean(x * x, axis=-1, keepdims=True) + EPS)
    return (y * g) * (1.0 + scale) + shift


def _layer_norm(x, g, b):
    mu = jnp.mean(x, axis=-1, keepdims=True)
    xc = x - mu
    var = jnp.mean(xc * xc, axis=-1, keepdims=True)
    return xc * lax.rsqrt(var + EPS) * g + b


def _mod_row(i, tiles_per_batch, ctx_row):
    if tiles_per_batch is None:
        return ctx_row
    return i // tiles_per_batch


def _cparams(sem, vmem_mb):
    return pltpu.CompilerParams(dimension_semantics=sem, vmem_limit_bytes=vmem_mb * 1024 * 1024)


def _mod_kernel(c_ref, w_ref, b_ref, o_ref):
    c = c_ref[...]
    s = (c * _sigmoid(c)).astype(BF16)
    o_ref[...] = _dot(s, w_ref[...].astype(BF16)) + b_ref[...]


def _mod_table(cc, w_mod, b_mod):
    depth = w_mod.shape[0]
    tn = 1024
    n_cols = N_MOD * D_MODEL
    return pl.pallas_call(
        _mod_kernel,
        out_shape=jax.ShapeDtypeStruct((depth, MOD_ROWS, n_cols), F32),
        grid=(depth, n_cols // tn),
        in_specs=[
            pl.BlockSpec((MOD_ROWS, D_MODEL), lambda l, j: (0, 0)),
            pl.BlockSpec((None, D_MODEL, tn), lambda l, j: (l, 0, j)),
            pl.BlockSpec((None, 1, tn), lambda l, j: (l, 0, j)),
        ],
        out_specs=pl.BlockSpec((None, MOD_ROWS, tn), lambda l, j: (l, 0, j)),
        compiler_params=_cparams(("arbitrary", "arbitrary"), 40),
        name="mod_table",
    )(cc, w_mod, b_mod.reshape(depth, 1, n_cols))


def _ffn_kernel(tiles_per_batch, ctx_row, n_f, final, *refs):
    if final:
        (x_ref, sh_ref, sc_ref, gt_ref, g_ref, wg_ref, wu_ref, wd_ref, fg_ref,
         o_ref, h_ref) = refs
    else:
        (x_ref, sh_ref, sc_ref, gt_ref, g_ref, wg_ref, wu_ref, wd_ref,
         o_ref, h_ref) = refs
    i = pl.program_id(0)
    f = pl.program_id(1)
    row = _mod_row(i, tiles_per_batch, ctx_row)

    tm = x_ref.shape[0]
    chunks = [slice(r, r + FFN_ROW_CHUNK) for r in range(0, tm, FFN_ROW_CHUNK)]

    def swiglu(h):
        g = _dot(h, wg_ref[...])
        u = _dot(h, wu_ref[...])
        return _dot(((g * _sigmoid(g)) * u).astype(BF16), wd_ref[...])

    @pl.when(f == 0)
    def _():
        g = g_ref[...]
        shift, scale = sh_ref[pl.ds(row, 1), :], sc_ref[pl.ds(row, 1), :]
        for rows in chunks:
            h = _rms_mod(x_ref[rows, :], g, shift, scale).astype(BF16)
            h_ref[rows, :] = h
            o_ref[rows, :] = swiglu(h)

    @pl.when((f > 0) & (f < n_f - 1))
    def _():
        o_ref[...] += swiglu(h_ref[...])

    @pl.when(f == n_f - 1)
    def _():
        gate = 0.5 * gt_ref[pl.ds(row, 1), :]
        for rows in chunks:
            y = x_ref[rows, :] + gate * (o_ref[rows, :] + swiglu(h_ref[rows, :]))
            if final:
                y = y * lax.rsqrt(jnp.mean(y * y, axis=-1, keepdims=True) + EPS) * fg_ref[...]
            o_ref[rows, :] = y


def _ffn(s, mod, layer, j, norm_g, w_gu, w_down, *, seq, ctx_row, final_g=None):
    rows = s.shape[0]
    tm, tf = min(TM_FFN, rows), TF_FFN
    n_f = D_FF // tf
    tiles_per_batch = None if seq is None else seq // tm
    final = final_g is not None
    mod_spec = lambda k: pl.BlockSpec((None, MOD_ROWS, D_MODEL), lambda i, f: (layer, 0, 3 * j + k))
    in_specs = [
        pl.BlockSpec((tm, D_MODEL), lambda i, f: (i, 0)),
        mod_spec(0), mod_spec(1), mod_spec(2),
        pl.BlockSpec((None, 1, D_MODEL), lambda i, f: (layer, 0, 0)),
        pl.BlockSpec((None, D_MODEL, tf), lambda i, f: (layer, 0, f)),
        pl.BlockSpec((None, D_MODEL, tf), lambda i, f: (layer, 0, n_f + f)),
        pl.BlockSpec((None, tf, D_MODEL), lambda i, f: (layer, f, 0)),
    ]
    args = [s, mod, mod, mod, norm_g, w_gu, w_gu, w_down]
    if final:
        in_specs.append(pl.BlockSpec((1, D_MODEL), lambda i, f: (0, 0)))
        args.append(final_g)
    return pl.pallas_call(
        functools.partial(_ffn_kernel, tiles_per_batch, ctx_row, n_f, final),
        out_shape=jax.ShapeDtypeStruct((rows, D_MODEL), F32),
        grid=(rows // tm, n_f),
        in_specs=in_specs,
        out_specs=pl.BlockSpec((tm, D_MODEL), lambda i, f: (i, 0)),
        scratch_shapes=[pltpu.VMEM((tm, D_MODEL), BF16)],
        compiler_params=_cparams(("arbitrary", "arbitrary"), 56),
        name="ffn",
    )(*args)


def _gelu_tanh(x):
    c = math.sqrt(2.0 / math.pi)
    return x * (0.5 * (1.0 + jnp.tanh(c * (x + 0.044715 * (x * x * x)))))


def _inproj_kernel(tiles_per_batch, ctx_row, x_ref, sh_ref, sc_ref, g_ref, w_ref, lng_ref, lnb_ref,
                   ws_ref, bst_ref, a_ref, gm_ref, qkv_ref, gate_ref, h_ref, pa_ref, u_ref):
    i = pl.program_id(0)
    j = pl.program_id(1)
    row = _mod_row(i, tiles_per_batch, ctx_row)
    tm = x_ref.shape[0]

    @pl.when(j == 0)
    def _():
        h = _rms_mod(x_ref[...], g_ref[...], sh_ref[pl.ds(row, 1), :], sc_ref[pl.ds(row, 1), :])
        h_ref[...] = h.astype(BF16)

    def proj():
        return _dot(h_ref[...], w_ref[...])

    @pl.when(j == 0)
    def _():
        pa_ref[...] = proj()

    @pl.when(j == 1)
    def _():
        a_ref[...] = (pa_ref[...] * _sigmoid(proj())).astype(BF16)

    @pl.when(j == 2)
    def _():
        u_ref[...] = _gelu_tanh(proj())

    @pl.when(j == 3)
    def _():
        vn = _layer_norm(_gelu_tanh(proj()), lng_ref[...], lnb_ref[...]).astype(BF16)
        gw = GMLP_DIM // GMLP_GROUPS
        for c in range(tm // GMLP_CHUNK):
            r0 = c * GMLP_CHUNK
            for g in range(GMLP_GROUPS):
                sp = _dot(ws_ref[g], vn[r0:r0 + GMLP_CHUNK, g * gw:(g + 1) * gw]) + bst_ref[:, g:g + 1]
                gm_ref[r0:r0 + GMLP_CHUNK, g * gw:(g + 1) * gw] = (
                    u_ref[r0:r0 + GMLP_CHUNK, g * gw:(g + 1) * gw] * sp).astype(BF16)

    @pl.when(j == 4)
    def _():
        qkv_ref[...] = (proj() * (HEAD_DIM ** -0.5)).astype(BF16)

    @pl.when((j == 5) | (j == 6))
    def _():
        qkv_ref[...] = proj().astype(BF16)

    @pl.when(j >= 7)
    def _():
        gate_ref[...] = _sigmoid(proj()).astype(BF16)


def _inproj(s, mod, layer, norm_g, w_in, ln_g, ln_b, ws, bst, *, seq, ctx_row):
    rows = s.shape[0]
    tm, tn = TM_IN, TN_IN
    n_j = IN_COLS // tn
    j_q, j_gate = O_Q // tn, O_GATE // tn
    tiles_per_batch = None if seq is None else seq // tm
    mod_spec = lambda k: pl.BlockSpec((None, MOD_ROWS, D_MODEL), lambda i, j: (layer, 0, 3 + k))
    vec = lambda n: pl.BlockSpec((None, 1, n), lambda i, j: (layer, 0, 0))
    return pl.pallas_call(
        functools.partial(_inproj_kernel, tiles_per_batch, ctx_row),
        out_shape=(
            jax.ShapeDtypeStruct((rows, CONV_DIM), BF16),
            jax.ShapeDtypeStruct((rows, GMLP_DIM), BF16),
            jax.ShapeDtypeStruct((rows, 3 * ATTN_DIM), BF16),
            jax.ShapeDtypeStruct((rows, 3 * D_MODEL), BF16),
        ),
        grid=(rows // tm, n_j),
        in_specs=[
            pl.BlockSpec((tm, D_MODEL), lambda i, j: (i, 0)),
            mod_spec(0), mod_spec(1),
            vec(D_MODEL),
            pl.BlockSpec((None, D_MODEL, tn), lambda i, j: (layer, 0, j)),
            vec(GMLP_DIM), vec(GMLP_DIM),
            pl.BlockSpec((None, GMLP_GROUPS, GMLP_CHUNK, GMLP_CHUNK), lambda i, j: (layer, 0, 0, 0)),
            pl.BlockSpec((None, GMLP_CHUNK, GMLP_GROUPS), lambda i, j: (layer, 0, 0)),
        ],
        out_specs=(
            pl.BlockSpec((tm, CONV_DIM), lambda i, j: (i, 0)),
            pl.BlockSpec((tm, GMLP_DIM), lambda i, j: (i, 0)),
            pl.BlockSpec((tm, tn), lambda i, j: (i, jnp.clip(j - j_q, 0, 2))),
            pl.BlockSpec((tm, tn), lambda i, j: (i, jnp.clip(j - j_gate, 0, n_j - j_gate - 1))),
        ),
        scratch_shapes=[pltpu.VMEM((tm, D_MODEL), BF16), pltpu.VMEM((tm, CONV_DIM), F32),
                        pltpu.VMEM((tm, GMLP_DIM), F32)],
        compiler_params=_cparams(("arbitrary", "arbitrary"), 48),
        name="inproj",
    )(s, mod, mod, norm_g, w_in, ln_g, ln_b, ws, bst)


def _conv_kernel(tiles_per_seq, prev_ref, cur_ref, next_ref, dw_ref, db_ref, lng_ref, lnb_ref,
                 o_ref, win_ref, sh_ref, acc_ref):
    tq = cur_ref.shape[0]
    pos = pl.program_id(0) % tiles_per_seq
    win_ref[0:HALO, :] = jnp.where(pos > 0, prev_ref[...].astype(F32), 0.0)
    win_ref[HALO:HALO + tq, :] = cur_ref[...].astype(F32)
    win_ref[HALO + tq:HALO + tq + HALO, :] = jnp.where(pos < tiles_per_seq - 1, next_ref[...].astype(F32), 0.0)
    n_lb = CONV_DIM // V7X_LANES
    n_sh = sh_ref.shape[2]
    for r in range(8):
        for lb in range(n_lb):
            sh_ref[r, lb] = win_ref[r:r + n_sh, lb * V7X_LANES:(lb + 1) * V7X_LANES]
    rb = CONV_ROW_BLOCK
    first = HALO - CONV_WIDTH // 2

    def body(it, carry):
        t, lb = it // n_lb, it % n_lb
        base = pl.multiple_of(t * rb, rb)
        accs = [None, None]
        for k in range(CONV_WIDTH):
            off = first + k
            term = sh_ref[off % 8, lb, pl.ds(base + (off // 8) * 8, rb), :] * dw_ref[k, pl.ds(lb, 1), :]
            accs[k % 2] = term if accs[k % 2] is None else accs[k % 2] + term
        acc_ref[lb, pl.ds(base, rb), :] = accs[0] + accs[1]
        return carry

    lax.fori_loop(0, (tq // rb) * n_lb, body, 0)
    acc = jnp.concatenate([acc_ref[lb] for lb in range(n_lb)], axis=1)
    y = _layer_norm(acc + db_ref[...], lng_ref[...], lnb_ref[...])
    o_ref[...] = (y * _sigmoid(y)).astype(BF16)


def _conv(a, layer, dw, db, ln_g, ln_b, *, seq):
    rows = a.shape[0]
    tq = TQ_CONV
    tiles_per_seq = seq // tq
    halo_blocks = rows // HALO
    per = tq // HALO
    vec = pl.BlockSpec((None, 1, CONV_DIM), lambda t: (layer, 0, 0))
    n_sh = tq + 2 * HALO - 8
    n_lb = CONV_DIM // V7X_LANES
    return pl.pallas_call(
        functools.partial(_conv_kernel, tiles_per_seq),
        out_shape=jax.ShapeDtypeStruct((rows, CONV_DIM), BF16),
        grid=(rows // tq,),
        in_specs=[
            pl.BlockSpec((HALO, CONV_DIM), lambda t: (jnp.maximum(t * per - 1, 0), 0)),
            pl.BlockSpec((tq, CONV_DIM), lambda t: (t, 0)),
            pl.BlockSpec((HALO, CONV_DIM), lambda t: (jnp.minimum((t + 1) * per, halo_blocks - 1), 0)),
            pl.BlockSpec((None, CONV_WIDTH, n_lb, V7X_LANES), lambda t: (layer, 0, 0, 0)),
            vec, vec, vec,
        ],
        out_specs=pl.BlockSpec((tq, CONV_DIM), lambda t: (t, 0)),
        scratch_shapes=[pltpu.VMEM((tq + 2 * HALO, CONV_DIM), F32),
                        pltpu.VMEM((8, n_lb, n_sh, V7X_LANES), F32),
                        pltpu.VMEM((n_lb, tq, V7X_LANES), F32)],
        compiler_params=_cparams(("arbitrary",), 40),
        name="conv",
    )(a, a, a, dw.reshape(dw.shape[0], CONV_WIDTH, n_lb, V7X_LANES), db, ln_g, ln_b)


def _softmax_pv(s, v):
    p = jnp.exp(s - s.max(axis=-1, keepdims=True))
    return _dot(p.astype(BF16), v) / p.sum(axis=-1, keepdims=True)


def _nbr_attn_kernel(n_b, n_g, plan_ref, q_ref, k0_ref, k1_ref, k2_ref, v0_ref, v1_ref, v2_ref,
                     kc_ref, vc_ref, tab_ref, o_ref, bias_ref):
    lo = lax.broadcasted_iota(jnp.int32, (QB, HP), 1) < HEAD_DIM
    zero = jnp.zeros((QB, HP), BF16)

    g = pl.program_id(1)
    kind = jnp.where(g == 0, 0, jnp.where(g == n_g - 1, 2, 1))
    pairs = QB_ROWS // 2
    per_kind = 3 * pairs * QB_ROWS
    for hh in range(2):
        for t in range(3):
            for dr in range(QB_ROWS):
                for p in range(pairs):
                    e = plan_ref[kind * per_kind + (t * QB_ROWS + dr) * pairs + p]
                    c0 = t * QB + p * 2 * GRID_W
                    bias_ref[hh, 0, dr * GRID_W:(dr + 1) * GRID_W, c0:c0 + 2 * GRID_W] = tab_ref[hh, e]

    def body(b, carry):
        q = q_ref[b]
        k = jnp.concatenate([k0_ref[b], k1_ref[b], k2_ref[b], kc_ref[b]], axis=0)
        v = jnp.concatenate([v0_ref[b], v1_ref[b], v2_ref[b], vc_ref[b]], axis=0)
        outs = []
        for hh in range(2):
            qm = jnp.where(lo, q, zero) if hh == 0 else jnp.where(lo, zero, q)
            s = _dot_nt(qm, k)
            s = jnp.concatenate([s[:, :3 * QB] + bias_ref[hh, 0], s[:, 3 * QB:]], axis=1)
            outs.append(_softmax_pv(s, v))
        o_ref[b] = jnp.where(lo, outs[0], outs[1]).astype(BF16)
        return carry

    lax.fori_loop(0, n_b, body, 0, unroll=True)


def _nbr_attn(qkv, qkv_ctx, table, plan):
    n_b, seq, _ = qkv.shape
    ctx = qkv_ctx.shape[1]
    n_g = seq // QB
    n_hp = ATTN_DIM // HP
    n_e = table.shape[1]
    kb = lambda g: jnp.clip(g - 1, 0, n_g - 3)
    blk = (n_b, QB, HP)
    kspec = lambda t, col: pl.BlockSpec(blk, lambda hp, g, plan: (0, kb(g) + t, col * n_hp + hp))
    return pl.pallas_call(
        functools.partial(_nbr_attn_kernel, n_b, n_g),
        out_shape=jax.ShapeDtypeStruct((n_b, seq, ATTN_DIM), BF16),
        grid_spec=pltpu.PrefetchScalarGridSpec(
            num_scalar_prefetch=1,
            grid=(n_hp, n_g),
            in_specs=[
                pl.BlockSpec(blk, lambda hp, g, plan: (0, g, hp)),
                kspec(0, 1), kspec(1, 1), kspec(2, 1),
                kspec(0, 2), kspec(1, 2), kspec(2, 2),
                pl.BlockSpec((n_b, ctx, HP), lambda hp, g, plan: (0, 0, n_hp + hp)),
                pl.BlockSpec((n_b, ctx, HP), lambda hp, g, plan: (0, 0, 2 * n_hp + hp)),
                pl.BlockSpec((2, n_e, GRID_W, 2 * GRID_W), lambda hp, g, plan: (hp, 0, 0, 0)),
            ],
            out_specs=pl.BlockSpec(blk, lambda hp, g, plan: (0, g, hp)),
            scratch_shapes=[pltpu.VMEM((2, 1, QB, 3 * QB), F32)],
        ),
        compiler_params=_cparams(("arbitrary", "arbitrary"), 40),
        name="nbr_attn",
    )(plan, qkv, qkv, qkv, qkv, qkv, qkv, qkv, qkv_ctx, qkv_ctx, table)


def _ctx_attn_kernel(n_b, q_ref, k_ref, v_ref, o_ref):
    n_q = q_ref.shape[1]
    lo = lax.broadcasted_iota(jnp.int32, (n_q, HP), 1) < HEAD_DIM
    zero = jnp.zeros((n_q, HP), BF16)

    def body(b, carry):
        q = q_ref[b]
        outs = []
        for hh in range(2):
            qm = jnp.where(lo, q, zero) if hh == 0 else jnp.where(lo, zero, q)
            outs.append(_softmax_pv(_dot_nt(qm, k_ref[b]), v_ref[b]))
        o_ref[b] = jnp.where(lo, outs[0], outs[1]).astype(BF16)
        return carry

    lax.fori_loop(0, n_b, body, 0)


def _ctx_attn(qkv_ctx):
    n_b, ctx, _ = qkv_ctx.shape
    n_hp = ATTN_DIM // HP
    blk = (n_b, ctx, HP)
    return pl.pallas_call(
        functools.partial(_ctx_attn_kernel, n_b),
        out_shape=jax.ShapeDtypeStruct((n_b, ctx, ATTN_DIM), BF16),
        grid=(n_hp,),
        in_specs=[
            pl.BlockSpec(blk, lambda hp: (0, 0, hp)),
            pl.BlockSpec(blk, lambda hp: (0, 0, n_hp + hp)),
            pl.BlockSpec(blk, lambda hp: (0, 0, 2 * n_hp + hp)),
        ],
        out_specs=pl.BlockSpec(blk, lambda hp: (0, 0, hp)),
        compiler_params=_cparams(("arbitrary",), 32),
        name="ctx_attn",
    )(qkv_ctx, qkv_ctx, qkv_ctx)


def _attn_bias_plan(rows):
    n_g = rows // QB_ROWS
    masked = 2 * WIN_ROWS - 1
    pairs = {}
    plan = np.zeros((3, 3, QB_ROWS, QB_ROWS // 2), np.int32)
    for kind, g in enumerate((0, 1, n_g - 1)):
        r = g * QB_ROWS + np.arange(QB_ROWS)
        row_start = np.clip(r - WIN_ROWS // 2, 0, rows - WIN_ROWS)
        kr = min(max(g - 1, 0), n_g - 3) * QB_ROWS + np.arange(3 * QB_ROWS)
        row_ok = (kr[None, :] >= row_start[:, None]) & (kr[None, :] < row_start[:, None] + WIN_ROWS)
        row_off = kr[None, :] - r[:, None] + WIN_ROWS - 1
        assert np.all((row_off[row_ok] >= 0) & (row_off[row_ok] < masked))
        ro = np.where(row_ok, row_off, masked)
        for t in range(3):
            for dr in range(QB_ROWS):
                for p in range(QB_ROWS // 2):
                    k0 = t * QB_ROWS + 2 * p
                    key = (int(ro[dr, k0]), int(ro[dr, k0 + 1]))
                    plan[kind, t, dr, p] = pairs.setdefault(key, len(pairs))
    return tuple(pairs), plan.reshape(-1)


def _bias_table_kernel(pairs, rpb_ref, o_ref, rows_ref):
    h = pl.program_id(0)
    n_ro, n_co = 2 * WIN_ROWS - 1, 2 * WIN_COLS - 1
    shape = (GRID_W, 2 * GRID_W)
    qc = lax.broadcasted_iota(jnp.int32, shape, 0)
    lane = lax.broadcasted_iota(jnp.int32, shape, 1)
    kc = lane & (GRID_W - 1)
    col_start = jnp.clip(qc - WIN_COLS // 2, 0, GRID_W - WIN_COLS)
    off = jnp.clip(kc - qc, -(WIN_COLS - 1), WIN_COLS - 1) + WIN_COLS - 1
    off = jnp.where(kc >= col_start, jnp.where(kc < col_start + WIN_COLS, off, -1), -1)
    neg = jnp.full(shape, NEG_INF, F32)
    for ro in range(n_ro):
        val = neg
        for j in range(n_co):
            val = jnp.where(off == j, rpb_ref[(h * n_ro + ro) * n_co + j], val)
        rows_ref[ro] = val
    rows_ref[n_ro] = neg
    for e, (a, b) in enumerate(pairs):
        o_ref[e] = jnp.where(lane < GRID_W, rows_ref[a], rows_ref[b])


def _bias_table(rpb_flat, pairs):
    n_e = len(pairs)
    return pl.pallas_call(
        functools.partial(_bias_table_kernel, pairs),
        out_shape=jax.ShapeDtypeStruct((N_HEADS, n_e, GRID_W, 2 * GRID_W), F32),
        grid=(N_HEADS,),
        in_specs=[pl.BlockSpec(memory_space=pltpu.SMEM)],
        out_specs=pl.BlockSpec((None, n_e, GRID_W, 2 * GRID_W), lambda h: (h, 0, 0, 0)),
        scratch_shapes=[pltpu.VMEM((2 * WIN_ROWS, GRID_W, 2 * GRID_W), F32)],
        compiler_params=_cparams(("arbitrary",), 16),
        name="bias_table",
    )(rpb_flat)


def _outproj_kernel(tiles_per_batch, ctx_row, conv_ref, gm_ref, at_ref, gate_ref, x_ref, mg_ref,
                    wc_ref, wg_ref, wa_ref, wo_ref, o_ref):
    i = pl.program_id(0)
    row = _mod_row(i, tiles_per_batch, ctx_row)
    y = gate_ref[:, 0:D_MODEL].astype(F32) * _dot(conv_ref[...], wc_ref[...])
    y = y + gate_ref[:, D_MODEL:2 * D_MODEL].astype(F32) * _dot(gm_ref[...], wg_ref[...])
    y = y + gate_ref[:, 2 * D_MODEL:3 * D_MODEL].astype(F32) * _dot(at_ref[...], wa_ref[...])
    o_ref[...] = x_ref[...] + mg_ref[pl.ds(row, 1), :] * _dot(y.astype(BF16), wo_ref[...])


def _outproj(conv, gm, attn, gates, s, mod, layer, wc, wg, wa, wo, *, seq, ctx_row):
    rows = s.shape[0]
    tm = TM_OUT
    tiles_per_batch = None if seq is None else seq // tm
    act = lambda n: pl.BlockSpec((tm, n), lambda i: (i, 0))
    wspec = lambda k: pl.BlockSpec((None, k, D_MODEL), lambda i: (layer, 0, 0),
                                   pipeline_mode=pl.Buffered(1))
    return pl.pallas_call(
        functools.partial(_outproj_kernel, tiles_per_batch, ctx_row),
        out_shape=jax.ShapeDtypeStruct((rows, D_MODEL), F32),
        grid=(rows // tm,),
        in_specs=[
            act(CONV_DIM), act(GMLP_DIM), act(ATTN_DIM), act(3 * D_MODEL), act(D_MODEL),
            pl.BlockSpec((None, MOD_ROWS, D_MODEL), lambda i: (layer, 0, 5)),
            wspec(CONV_DIM), wspec(GMLP_DIM), wspec(ATTN_DIM), wspec(D_MODEL),
        ],
        out_specs=act(D_MODEL),
        compiler_params=_cparams(("arbitrary",), 52),
        name="outproj",
    )(conv, gm, attn, gates, s, mod, wc, wg, wa, wo)


def kernel(x, c, ctx, c_ctx, w_mod, b_mod, norm_ffn1, ffn1_w_gu, ffn1_w_down, norm_mix, w_in, conv_dw, conv_db, conv_ln_g, conv_ln_b, gmlp_ln_g, gmlp_ln_b, gmlp_ws, gmlp_bs, attn_rpb, w_conv_out, w_gmlp_out, w_attn_out, w_out, norm_ffn2, ffn2_w_gu, ffn2_w_down, final_norm):
    n_b, seq, d = x.shape
    n_ctx = ctx.shape[1]
    depth = w_mod.shape[0]
    assert d == D_MODEL and n_b < MOD_ROWS and seq % QB == 0 and seq // QB >= 3
    assert seq % TM_FFN == 0 and (n_b * n_ctx) % TM_IN == 0 and n_ctx % TQ_CONV == 0
    ctx_row = n_b

    cc = jnp.concatenate([c, c_ctx[None, :], jnp.zeros((MOD_ROWS - n_b - 1, d), F32)], axis=0)
    mod = _mod_table(cc, w_mod, b_mod)

    vec3 = lambda a: a.reshape(depth, 1, a.shape[-1])
    norm_ffn1, norm_mix, norm_ffn2 = vec3(norm_ffn1), vec3(norm_mix), vec3(norm_ffn2)
    conv_db, conv_ln_g, conv_ln_b = vec3(conv_db), vec3(conv_ln_g), vec3(conv_ln_b)
    gmlp_ln_g, gmlp_ln_b = vec3(gmlp_ln_g), vec3(gmlp_ln_b)
    bst = jnp.swapaxes(gmlp_bs, 1, 2)
    ws = gmlp_ws.astype(BF16)
    w1gu, w1d = ffn1_w_gu.astype(BF16), ffn1_w_down.astype(BF16)
    w2gu, w2d = ffn2_w_gu.astype(BF16), ffn2_w_down.astype(BF16)
    w_in_b = w_in.astype(BF16)
    wc, wg, wa, wo = (w.astype(BF16) for w in (w_conv_out, w_gmlp_out, w_attn_out, w_out))
    assert 2 * GRID_W == V7X_LANES and QB_ROWS % 2 == 0
    pairs, plan = _attn_bias_plan(seq // GRID_W)
    plan = jnp.asarray(plan)
    rpb_flat = attn_rpb.reshape(depth, -1)
    final_g = final_norm.reshape(1, d)

    xs = x.reshape(n_b * seq, d)
    cs = ctx.reshape(n_b * n_ctx, d)
    lat = dict(seq=seq, ctx_row=ctx_row)
    cxt = dict(seq=None, ctx_row=ctx_row)

    for i in range(depth):
        last = i == depth - 1
        xs = _ffn(xs, mod, i, 0, norm_ffn1, w1gu, w1d, **lat)
        cs = _ffn(cs, mod, i, 0, norm_ffn1, w1gu, w1d, **cxt)
        a_x, gm_x, qkv_x, gate_x = _inproj(xs, mod, i, norm_mix, w_in_b, gmlp_ln_g, gmlp_ln_b, ws, bst, **lat)
        a_c, gm_c, qkv_c, gate_c = _inproj(cs, mod, i, norm_mix, w_in_b, gmlp_ln_g, gmlp_ln_b, ws, bst, **cxt)
        qkv_c3 = qkv_c.reshape(n_b, n_ctx, 3 * ATTN_DIM)
        table = _bias_table(rpb_flat[i], pairs)
        attn_x = _nbr_attn(qkv_x.reshape(n_b, seq, 3 * ATTN_DIM), qkv_c3, table, plan)
        conv_x = _conv(a_x, i, conv_dw, conv_db, conv_ln_g, conv_ln_b, seq=seq)
        xs = _outproj(conv_x, gm_x, attn_x.reshape(n_b * seq, ATTN_DIM), gate_x, xs, mod, i,
                      wc, wg, wa, wo, **lat)
        xs = _ffn(xs, mod, i, 2, norm_ffn2, w2gu, w2d, final_g=final_g if last else None, **lat)
        if not last:
            attn_c = _ctx_attn(qkv_c3)
            conv_c = _conv(a_c, i, conv_dw, conv_db, conv_ln_g, conv_ln_b, seq=n_ctx)
            cs = _outproj(conv_c, gm_c, attn_c.reshape(n_b * n_ctx, ATTN_DIM), gate_c, cs, mod, i,
                          wc, wg, wa, wo, **cxt)
            cs = _ffn(cs, mod, i, 2, norm_ffn2, w2gu, w2d, **cxt)
    return xs.reshape(n_b, seq, d)
```

```python
import functools
import math

import numpy as np
import jax
import jax.numpy as jnp
from jax import lax
from jax.experimental import pallas as pl
from jax.experimental.pallas import tpu as pltpu

D_MODEL = 2048
D_FF = 5632
CONV_DIM = 1024
CONV_WIDTH = 31
GMLP_DIM = 1024
GMLP_CHUNK = 128
GMLP_GROUPS = 4
N_HEADS = 16
HEAD_DIM = 64
ATTN_DIM = N_HEADS * HEAD_DIM
GRID_W = 64
WIN_ROWS = 8
WIN_COLS = 16
N_MOD = 9
EPS = 1e-6
NEG_INF = -1e9

O_CONV = 0
O_GMLP = O_CONV + 2 * CONV_DIM
O_Q = O_GMLP + 2 * GMLP_DIM
O_GATE = O_Q + 3 * ATTN_DIM
IN_COLS = O_GATE + 3 * D_MODEL

MOD_ROWS = 8
V7X_LANES = 128
V7X_VMEM_BYTES = 64 * 1024 * 1024

F32 = jnp.float32
BF16 = jnp.bfloat16

TM_FFN = 1024
TM_FFN_MIX = 512
TF_FFN = 512
FFN_ROW_CHUNK = 256
TM_IN = 1024
IN_ROW_CHUNK = 256
TN_IN = 1024
TM_OUT = 256
TQ_CONV = 256
HALO = 16
CONV_ROW_BLOCK = 64
QB_ROWS = 4
QB = QB_ROWS * GRID_W
HP = 2 * HEAD_DIM


def _sigmoid(x):
    return 1.0 / (1.0 + jnp.exp(-x))


def _dot(a, b):
    return jnp.dot(a, b, preferred_element_type=F32)


def _dot_nt(a, b):
    return lax.dot_general(a, b, (((1,), (1,)), ((), ())), preferred_element_type=F32)


def _rms_mod(x, g, shift, scale):
    y = x * lax.rsqrt(jnp.mean(x * x, axis=-1, keepdims=True) + EPS)
    return (y * g) * (1.0 + scale) + shift


def _layer_norm(x, g, b):
    mu = jnp.mean(x, axis=-1, keepdims=True)
    xc = x - mu
    var = jnp.mean(xc * xc, axis=-1, keepdims=True)
    return xc * lax.rsqrt(var + EPS) * g + b


def _mod_row(i, tiles_per_batch, ctx_row):
    if tiles_per_batch is None:
        return ctx_row
    return i // tiles_per_batch


def _cparams(sem, vmem_mb):
    return pltpu.CompilerParams(dimension_semantics=sem, vmem_limit_bytes=vmem_mb * 1024 * 1024)


def _mod_kernel(c_ref, w_ref, b_ref, o_ref):
    c = c_ref[...]
    s = (c * _sigmoid(c)).astype(BF16)
    o_ref[...] = _dot(s, w_ref[...].astype(BF16)) + b_ref[...]


def _mod_table(cc, w_mod, b_mod):
    depth = w_mod.shape[0]
    tn = 1024
    n_cols = N_MOD * D_MODEL
    return pl.pallas_call(
        _mod_kernel,
        out_shape=jax.ShapeDtypeStruct((depth, MOD_ROWS, n_cols), F32),
        grid=(depth, n_cols // tn),
        in_specs=[
            pl.BlockSpec((MOD_ROWS, D_MODEL), lambda l, j: (0, 0)),
            pl.BlockSpec((None, D_MODEL, tn), lambda l, j: (l, 0, j)),
            pl.BlockSpec((None, 1, tn), lambda l, j: (l, 0, j)),
        ],
        out_specs=pl.BlockSpec((None, MOD_ROWS, tn), lambda l, j: (l, 0, j)),
        compiler_params=_cparams(("arbitrary", "arbitrary"), 40),
        name="mod_table",
    )(cc, w_mod, b_mod.reshape(depth, 1, n_cols))


def _ffn_kernel(tiles_per_batch, ctx_row, n_f, final, mix, *refs):
    refs = list(refs)
    x_ref, sh_ref, sc_ref, gt_ref, g_ref, wg_ref, wu_ref, wd_ref = refs[:8]
    del refs[:8]
    fg_ref = refs.pop(0) if final else None
    g2_ref, sh2_ref, sc2_ref = (refs.pop(0), refs.pop(0), refs.pop(0)) if mix else (None, None, None)
    o_ref = refs.pop(0)
    hm_ref = refs.pop(0) if mix else None
    (h_ref,) = refs
    i = pl.program_id(0)
    f = pl.program_id(1)
    row = _mod_row(i, tiles_per_batch, ctx_row)

    tm = x_ref.shape[0]
    chunks = [slice(r, r + FFN_ROW_CHUNK) for r in range(0, tm, FFN_ROW_CHUNK)]

    def swiglu(h):
        g = _dot(h, wg_ref[...])
        u = _dot(h, wu_ref[...])
        return _dot(((g * _sigmoid(g)) * u).astype(BF16), wd_ref[...])

    @pl.when(f == 0)
    def _():
        g = g_ref[...]
        shift, scale = sh_ref[pl.ds(row, 1), :], sc_ref[pl.ds(row, 1), :]
        for rows in chunks:
            h = _rms_mod(x_ref[rows, :], g, shift, scale).astype(BF16)
            h_ref[rows, :] = h
            o_ref[rows, :] = swiglu(h)

    @pl.when((f > 0) & (f < n_f - 1))
    def _():
        o_ref[...] += swiglu(h_ref[...])

    @pl.when(f == n_f - 1)
    def _():
        gate = 0.5 * gt_ref[pl.ds(row, 1), :]
        if mix:
            g2 = g2_ref[...]
            shift2, scale2 = sh2_ref[pl.ds(row, 1), :], sc2_ref[pl.ds(row, 1), :]
        for rows in chunks:
            y = x_ref[rows, :] + gate * (o_ref[rows, :] + swiglu(h_ref[rows, :]))
            if final:
                y = y * lax.rsqrt(jnp.mean(y * y, axis=-1, keepdims=True) + EPS) * fg_ref[...]
            o_ref[rows, :] = y
            if mix:
                hm_ref[rows, :] = _rms_mod(y, g2, shift2, scale2).astype(BF16)


def _ffn(s, mod, layer, j, norm_g, w_gu, w_down, *, seq, ctx_row, final_g=None, mix_g=None):
    rows = s.shape[0]
    final, mix = final_g is not None, mix_g is not None
    tm, tf = min(TM_FFN_MIX if mix else TM_FFN, rows), TF_FFN
    n_f = D_FF // tf
    tiles_per_batch = None if seq is None else seq // tm
    mod_spec = lambda k: pl.BlockSpec((None, MOD_ROWS, D_MODEL), lambda i, f: (layer, 0, k))
    vec = pl.BlockSpec((None, 1, D_MODEL), lambda i, f: (layer, 0, 0))
    tile = pl.BlockSpec((tm, D_MODEL), lambda i, f: (i, 0))
    in_specs = [
        tile,
        mod_spec(3 * j), mod_spec(3 * j + 1), mod_spec(3 * j + 2),
        vec,
        pl.BlockSpec((None, D_MODEL, tf), lambda i, f: (layer, 0, f)),
        pl.BlockSpec((None, D_MODEL, tf), lambda i, f: (layer, 0, n_f + f)),
        pl.BlockSpec((None, tf, D_MODEL), lambda i, f: (layer, f, 0)),
    ]
    args = [s, mod, mod, mod, norm_g, w_gu, w_gu, w_down]
    out_shape = [jax.ShapeDtypeStruct((rows, D_MODEL), F32)]
    out_specs = [tile]
    if final:
        in_specs.append(pl.BlockSpec((1, D_MODEL), lambda i, f: (0, 0)))
        args.append(final_g)
    if mix:
        in_specs += [vec, mod_spec(3), mod_spec(4)]
        args += [mix_g, mod, mod]
        out_shape.append(jax.ShapeDtypeStruct((rows, D_MODEL), BF16))
        out_specs.append(tile)
    outs = pl.pallas_call(
        functools.partial(_ffn_kernel, tiles_per_batch, ctx_row, n_f, final, mix),
        out_shape=out_shape,
        grid=(rows // tm, n_f),
        in_specs=in_specs,
        out_specs=out_specs,
        scratch_shapes=[pltpu.VMEM((tm, D_MODEL), BF16)],
        compiler_params=_cparams(("arbitrary", "arbitrary"), 56),
        name="ffn",
    )(*args)
    return outs if mix else outs[0]


def _gelu_tanh(x):
    c = math.sqrt(2.0 / math.pi)
    return x * (0.5 * (1.0 + jnp.tanh(c * (x + 0.044715 * (x * x * x)))))


def _inproj_kernel(h_ref, w_ref, lng_ref, lnb_ref, ws_ref, bst_ref,
                   a_ref, gm_ref, qkv_ref, gate_ref, pa_ref, u_ref):
    j = pl.program_id(1)
    tm = h_ref.shape[0]
    chunks = [slice(r, r + IN_ROW_CHUNK) for r in range(0, tm, IN_ROW_CHUNK)]

    def proj(rows):
        return _dot(h_ref[rows, :], w_ref[...])

    @pl.when(j == 0)
    def _():
        for rows in chunks:
            pa_ref[rows, :] = proj(rows)

    @pl.when(j == 1)
    def _():
        for rows in chunks:
            a_ref[rows, :] = (pa_ref[rows, :] * _sigmoid(proj(rows))).astype(BF16)

    @pl.when(j == 2)
    def _():
        for rows in chunks:
            u_ref[rows, :] = _gelu_tanh(proj(rows))

    @pl.when(j == 3)
    def _():
        gw = GMLP_DIM // GMLP_GROUPS
        for rows in chunks:
            vn = _layer_norm(_gelu_tanh(proj(rows)), lng_ref[...], lnb_ref[...]).astype(BF16)
            for c in range(IN_ROW_CHUNK // GMLP_CHUNK):
                r0 = rows.start + c * GMLP_CHUNK
                for g in range(GMLP_GROUPS):
                    v_cg = vn[c * GMLP_CHUNK:(c + 1) * GMLP_CHUNK, g * gw:(g + 1) * gw]
                    sp = _dot(ws_ref[g], v_cg) + bst_ref[:, g:g + 1]
                    gm_ref[r0:r0 + GMLP_CHUNK, g * gw:(g + 1) * gw] = (
                        u_ref[r0:r0 + GMLP_CHUNK, g * gw:(g + 1) * gw] * sp).astype(BF16)

    @pl.when(j == 4)
    def _():
        for rows in chunks:
            qkv_ref[rows, :] = (proj(rows) * (HEAD_DIM ** -0.5)).astype(BF16)

    @pl.when((j == 5) | (j == 6))
    def _():
        for rows in chunks:
            qkv_ref[rows, :] = proj(rows).astype(BF16)

    @pl.when(j >= 7)
    def _():
        for rows in chunks:
            gate_ref[rows, :] = _sigmoid(proj(rows)).astype(BF16)


def _inproj(h, layer, w_in, ln_g, ln_b, ws, bst):
    rows = h.shape[0]
    tm, tn = min(TM_IN, rows), TN_IN
    n_j = IN_COLS // tn
    j_q, j_gate = O_Q // tn, O_GATE // tn
    vec = lambda n: pl.BlockSpec((None, 1, n), lambda i, j: (layer, 0, 0))
    return pl.pallas_call(
        _inproj_kernel,
        out_shape=(
            jax.ShapeDtypeStruct((rows, CONV_DIM), BF16),
            jax.ShapeDtypeStruct((rows, GMLP_DIM), BF16),
            jax.ShapeDtypeStruct((rows, 3 * ATTN_DIM), BF16),
            jax.ShapeDtypeStruct((rows, 3 * D_MODEL), BF16),
        ),
        grid=(rows // tm, n_j),
        in_specs=[
            pl.BlockSpec((tm, D_MODEL), lambda i, j: (i, 0)),
            pl.BlockSpec((None, D_MODEL, tn), lambda i, j: (layer, 0, j)),
            vec(GMLP_DIM), vec(GMLP_DIM),
            pl.BlockSpec((None, GMLP_GROUPS, GMLP_CHUNK, GMLP_CHUNK), lambda i, j: (layer, 0, 0, 0)),
            pl.BlockSpec((None, GMLP_CHUNK, GMLP_GROUPS), lambda i, j: (layer, 0, 0)),
        ],
        out_specs=(
            pl.BlockSpec((tm, CONV_DIM), lambda i, j: (i, 0)),
            pl.BlockSpec((tm, GMLP_DIM), lambda i, j: (i, 0)),
            pl.BlockSpec((tm, tn), lambda i, j: (i, jnp.clip(j - j_q, 0, 2))),
            pl.BlockSpec((tm, tn), lambda i, j: (i, jnp.clip(j - j_gate, 0, n_j - j_gate - 1))),
        ),
        scratch_shapes=[pltpu.VMEM((tm, CONV_DIM), F32), pltpu.VMEM((tm, GMLP_DIM), F32)],
        compiler_params=_cparams(("arbitrary", "arbitrary"), 56),
        name="inproj",
    )(h, w_in, ln_g, ln_b, ws, bst)


def _conv_kernel(tiles_per_seq, prev_ref, cur_ref, next_ref, dw_ref, db_ref, lng_ref, lnb_ref,
                 o_ref, win_ref, sh_ref, acc_ref):
    tq = cur_ref.shape[0]
    pos = pl.program_id(0) % tiles_per_seq
    win_ref[0:HALO, :] = jnp.where(pos > 0, prev_ref[...].astype(F32), 0.0)
    win_ref[HALO:HALO + tq, :] = cur_ref[...].astype(F32)
    win_ref[HALO + tq:HALO + tq + HALO, :] = jnp.where(pos < tiles_per_seq - 1, next_ref[...].astype(F32), 0.0)
    n_lb = CONV_DIM // V7X_LANES
    n_sh = sh_ref.shape[2]
    for r in range(8):
        for lb in range(n_lb):
            sh_ref[r, lb] = win_ref[r:r + n_sh, lb * V7X_LANES:(lb + 1) * V7X_LANES]
    rb = CONV_ROW_BLOCK
    first = HALO - CONV_WIDTH // 2

    def body(it, carry):
        t, lb = it // n_lb, it % n_lb
        base = pl.multiple_of(t * rb, rb)
        accs = [None, None]
        for k in range(CONV_WIDTH):
            off = first + k
            term = sh_ref[off % 8, lb, pl.ds(base + (off // 8) * 8, rb), :] * dw_ref[k, pl.ds(lb, 1), :]
            accs[k % 2] = term if accs[k % 2] is None else accs[k % 2] + term
        acc_ref[lb, pl.ds(base, rb), :] = accs[0] + accs[1]
        return carry

    lax.fori_loop(0, (tq // rb) * n_lb, body, 0)
    acc = jnp.concatenate([acc_ref[lb] for lb in range(n_lb)], axis=1)
    y = _layer_norm(acc + db_ref[...], lng_ref[...], lnb_ref[...])
    o_ref[...] = (y * _sigmoid(y)).astype(BF16)


def _conv(a, layer, dw, db, ln_g, ln_b, *, seq):
    rows = a.shape[0]
    tq = TQ_CONV
    tiles_per_seq = seq // tq
    halo_blocks = rows // HALO
    per = tq // HALO
    vec = pl.BlockSpec((None, 1, CONV_DIM), lambda t: (layer, 0, 0))
    n_sh = tq + 2 * HALO - 8
    n_lb = CONV_DIM // V7X_LANES
    return pl.pallas_call(
        functools.partial(_conv_kernel, tiles_per_seq),
        out_shape=jax.ShapeDtypeStruct((rows, CONV_DIM), BF16),
        grid=(rows // tq,),
        in_specs=[
            pl.BlockSpec((HALO, CONV_DIM), lambda t: (jnp.maximum(t * per - 1, 0), 0)),
            pl.BlockSpec((tq, CONV_DIM), lambda t: (t, 0)),
            pl.BlockSpec((HALO, CONV_DIM), lambda t: (jnp.minimum((t + 1) * per, halo_blocks - 1), 0)),
            pl.BlockSpec((None, CONV_WIDTH, n_lb, V7X_LANES), lambda t: (layer, 0, 0, 0)),
            vec, vec, vec,
        ],
        out_specs=pl.BlockSpec((tq, CONV_DIM), lambda t: (t, 0)),
        scratch_shapes=[pltpu.VMEM((tq + 2 * HALO, CONV_DIM), F32),
                        pltpu.VMEM((8, n_lb, n_sh, V7X_LANES), F32),
                        pltpu.VMEM((n_lb, tq, V7X_LANES), F32)],
        compiler_params=_cparams(("arbitrary",), 40),
        name="conv",
    )(a, a, a, dw.reshape(dw.shape[0], CONV_WIDTH, n_lb, V7X_LANES), db, ln_g, ln_b)


def _softmax_pv(s, v):
    p = jnp.exp(s - s.max(axis=-1, keepdims=True))
    return _dot(p.astype(BF16), v) / p.sum(axis=-1, keepdims=True)


def _nbr_attn_kernel(n_b, n_g, plan_ref, q_ref, k0_ref, k1_ref, k2_ref, v0_ref, v1_ref, v2_ref,
                     kc_ref, vc_ref, tab_ref, o_ref, bias_ref):
    lo = lax.broadcasted_iota(jnp.int32, (QB, HP), 1) < HEAD_DIM
    zero = jnp.zeros((QB, HP), BF16)

    g = pl.program_id(1)
    kind = jnp.where(g == 0, 0, jnp.where(g == n_g - 1, 2, 1))
    pairs = QB_ROWS // 2
    per_kind = 3 * pairs * QB_ROWS
    for hh in range(2):
        for t in range(3):
            for dr in range(QB_ROWS):
                for p in range(pairs):
                    e = plan_ref[kind * per_kind + (t * QB_ROWS + dr) * pairs + p]
                    c0 = t * QB + p * 2 * GRID_W
                    bias_ref[hh, 0, dr * GRID_W:(dr + 1) * GRID_W, c0:c0 + 2 * GRID_W] = tab_ref[hh, e]

    def body(b, carry):
        q = q_ref[b]
        k = jnp.concatenate([k0_ref[b], k1_ref[b], k2_ref[b], kc_ref[b]], axis=0)
        v = jnp.concatenate([v0_ref[b], v1_ref[b], v2_ref[b], vc_ref[b]], axis=0)
        outs = []
        for hh in range(2):
            qm = jnp.where(lo, q, zero) if hh == 0 else jnp.where(lo, zero, q)
            s = _dot_nt(qm, k)
            s = jnp.concatenate([s[:, :3 * QB] + bias_ref[hh, 0], s[:, 3 * QB:]], axis=1)
            outs.append(_softmax_pv(s, v))
        o_ref[b] = jnp.where(lo, outs[0], outs[1]).astype(BF16)
        return carry

    lax.fori_loop(0, n_b, body, 0, unroll=True)


def _nbr_attn(qkv, qkv_ctx, table, plan):
    n_b, seq, _ = qkv.shape
    ctx = qkv_ctx.shape[1]
    n_g = seq // QB
    n_hp = ATTN_DIM // HP
    n_e = table.shape[1]
    kb = lambda g: jnp.clip(g - 1, 0, n_g - 3)
    blk = (n_b, QB, HP)
    kspec = lambda t, col: pl.BlockSpec(blk, lambda hp, g, plan: (0, kb(g) + t, col * n_hp + hp))
    return pl.pallas_call(
        functools.partial(_nbr_attn_kernel, n_b, n_g),
        out_shape=jax.ShapeDtypeStruct((n_b, seq, ATTN_DIM), BF16),
        grid_spec=pltpu.PrefetchScalarGridSpec(
            num_scalar_prefetch=1,
            grid=(n_hp, n_g),
            in_specs=[
                pl.BlockSpec(blk, lambda hp, g, plan: (0, g, hp)),
                kspec(0, 1), kspec(1, 1), kspec(2, 1),
                kspec(0, 2), kspec(1, 2), kspec(2, 2),
                pl.BlockSpec((n_b, ctx, HP), lambda hp, g, plan: (0, 0, n_hp + hp)),
                pl.BlockSpec((n_b, ctx, HP), lambda hp, g, plan: (0, 0, 2 * n_hp + hp)),
                pl.BlockSpec((2, n_e, GRID_W, 2 * GRID_W), lambda hp, g, plan: (hp, 0, 0, 0)),
            ],
            out_specs=pl.BlockSpec(blk, lambda hp, g, plan: (0, g, hp)),
            scratch_shapes=[pltpu.VMEM((2, 1, QB, 3 * QB), F32)],
        ),
        compiler_params=_cparams(("arbitrary", "arbitrary"), 40),
        name="nbr_attn",
    )(plan, qkv, qkv, qkv, qkv, qkv, qkv, qkv, qkv_ctx, qkv_ctx, table)


def _ctx_attn_kernel(n_b, q_ref, k_ref, v_ref, o_ref):
    n_q = q_ref.shape[1]
    lo = lax.broadcasted_iota(jnp.int32, (n_q, HP), 1) < HEAD_DIM
    zero = jnp.zeros((n_q, HP), BF16)

    def body(b, carry):
        q = q_ref[b]
        outs = []
        for hh in range(2):
            qm = jnp.where(lo, q, zero) if hh == 0 else jnp.where(lo, zero, q)
            outs.append(_softmax_pv(_dot_nt(qm, k_ref[b]), v_ref[b]))
        o_ref[b] = jnp.where(lo, outs[0], outs[1]).astype(BF16)
        return carry

    lax.fori_loop(0, n_b, body, 0)


def _ctx_attn(qkv_ctx):
    n_b, ctx, _ = qkv_ctx.shape
    n_hp = ATTN_DIM // HP
    blk = (n_b, ctx, HP)
    return pl.pallas_call(
        functools.partial(_ctx_attn_kernel, n_b),
        out_shape=jax.ShapeDtypeStruct((n_b, ctx, ATTN_DIM), BF16),
        grid=(n_hp,),
        in_specs=[
            pl.BlockSpec(blk, lambda hp: (0, 0, hp)),
            pl.BlockSpec(blk, lambda hp: (0, 0, n_hp + hp)),
            pl.BlockSpec(blk, lambda hp: (0, 0, 2 * n_hp + hp)),
        ],
        out_specs=pl.BlockSpec(blk, lambda hp: (0, 0, hp)),
        compiler_params=_cparams(("arbitrary",), 32),
        name="ctx_attn",
    )(qkv_ctx, qkv_ctx, qkv_ctx)


def _attn_bias_plan(rows):
    n_g = rows // QB_ROWS
    masked = 2 * WIN_ROWS - 1
    pairs = {}
    plan = np.zeros((3, 3, QB_ROWS, QB_ROWS // 2), np.int32)
    for kind, g in enumerate((0, 1, n_g - 1)):
        r = g * QB_ROWS + np.arange(QB_ROWS)
        row_start = np.clip(r - WIN_ROWS // 2, 0, rows - WIN_ROWS)
        kr = min(max(g - 1, 0), n_g - 3) * QB_ROWS + np.arange(3 * QB_ROWS)
        row_ok = (kr[None, :] >= row_start[:, None]) & (kr[None, :] < row_start[:, None] + WIN_ROWS)
        row_off = kr[None, :] - r[:, None] + WIN_ROWS - 1
        assert np.all((row_off[row_ok] >= 0) & (row_off[row_ok] < masked))
        ro = np.where(row_ok, row_off, masked)
        for t in range(3):
            for dr in range(QB_ROWS):
                for p in range(QB_ROWS // 2):
                    k0 = t * QB_ROWS + 2 * p
                    key = (int(ro[dr, k0]), int(ro[dr, k0 + 1]))
                    plan[kind, t, dr, p] = pairs.setdefault(key, len(pairs))
    return tuple(pairs), plan.reshape(-1)


def _bias_table_kernel(pairs, rpb_ref, o_ref, rows_ref):
    h = pl.program_id(0)
    n_ro, n_co = 2 * WIN_ROWS - 1, 2 * WIN_COLS - 1
    shape = (GRID_W, 2 * GRID_W)
    qc = lax.broadcasted_iota(jnp.int32, shape, 0)
    lane = lax.broadcasted_iota(jnp.int32, shape, 1)
    kc = lane & (GRID_W - 1)
    col_start = jnp.clip(qc - WIN_COLS // 2, 0, GRID_W - WIN_COLS)
    off = jnp.clip(kc - qc, -(WIN_COLS - 1), WIN_COLS - 1) + WIN_COLS - 1
    off = jnp.where(kc >= col_start, jnp.where(kc < col_start + WIN_COLS, off, -1), -1)
    neg = jnp.full(shape, NEG_INF, F32)
    for ro in range(n_ro):
        val = neg
        for j in range(n_co):
            val = jnp.where(off == j, rpb_ref[(h * n_ro + ro) * n_co + j], val)
        rows_ref[ro] = val
    rows_ref[n_ro] = neg
    for e, (a, b) in enumerate(pairs):
        o_ref[e] = jnp.where(lane < GRID_W, rows_ref[a], rows_ref[b])


def _bias_table(rpb_flat, pairs):
    n_e = len(pairs)
    return pl.pallas_call(
        functools.partial(_bias_table_kernel, pairs),
        out_shape=jax.ShapeDtypeStruct((N_HEADS, n_e, GRID_W, 2 * GRID_W), F32),
        grid=(N_HEADS,),
        in_specs=[pl.BlockSpec(memory_space=pltpu.SMEM)],
        out_specs=pl.BlockSpec((None, n_e, GRID_W, 2 * GRID_W), lambda h: (h, 0, 0, 0)),
        scratch_shapes=[pltpu.VMEM((2 * WIN_ROWS, GRID_W, 2 * GRID_W), F32)],
        compiler_params=_cparams(("arbitrary",), 16),
        name="bias_table",
    )(rpb_flat)


def _outproj_kernel(tiles_per_batch, ctx_row, conv_ref, gm_ref, at_ref, gate_ref, x_ref, mg_ref,
                    wc_ref, wg_ref, wa_ref, wo_ref, o_ref):
    i = pl.program_id(0)
    row = _mod_row(i, tiles_per_batch, ctx_row)
    y = gate_ref[:, 0:D_MODEL].astype(F32) * _dot(conv_ref[...], wc_ref[...])
    y = y + gate_ref[:, D_MODEL:2 * D_MODEL].astype(F32) * _dot(gm_ref[...], wg_ref[...])
    y = y + gate_ref[:, 2 * D_MODEL:3 * D_MODEL].astype(F32) * _dot(at_ref[...], wa_ref[...])
    o_ref[...] = x_ref[...] + mg_ref[pl.ds(row, 1), :] * _dot(y.astype(BF16), wo_ref[...])


def _outproj(conv, gm, attn, gates, s, mod, layer, wc, wg, wa, wo, *, seq, ctx_row):
    rows = s.shape[0]
    tm = TM_OUT
    tiles_per_batch = None if seq is None else seq // tm
    act = lambda n: pl.BlockSpec((tm, n), lambda i: (i, 0))
    wspec = lambda k: pl.BlockSpec((None, k, D_MODEL), lambda i: (layer, 0, 0),
                                   pipeline_mode=pl.Buffered(1))
    return pl.pallas_call(
        functools.partial(_outproj_kernel, tiles_per_batch, ctx_row),
        out_shape=jax.ShapeDtypeStruct((rows, D_MODEL), F32),
        grid=(rows // tm,),
        in_specs=[
            act(CONV_DIM), act(GMLP_DIM), act(ATTN_DIM), act(3 * D_MODEL), act(D_MODEL),
            pl.BlockSpec((None, MOD_ROWS, D_MODEL), lambda i: (layer, 0, 5)),
            wspec(CONV_DIM), wspec(GMLP_DIM), wspec(ATTN_DIM), wspec(D_MODEL),
        ],
        out_specs=act(D_MODEL),
        compiler_params=_cparams(("arbitrary",), 52),
        name="outproj",
    )(conv, gm, attn, gates, s, mod, wc, wg, wa, wo)


def kernel(x, c, ctx, c_ctx, w_mod, b_mod, norm_ffn1, ffn1_w_gu, ffn1_w_down, norm_mix, w_in, conv_dw, conv_db, conv_ln_g, conv_ln_b, gmlp_ln_g, gmlp_ln_b, gmlp_ws, gmlp_bs, attn_rpb, w_conv_out, w_gmlp_out, w_attn_out, w_out, norm_ffn2, ffn2_w_gu, ffn2_w_down, final_norm):
    n_b, seq, d = x.shape
    n_ctx = ctx.shape[1]
    depth = w_mod.shape[0]
    assert d == D_MODEL and n_b < MOD_ROWS and seq % QB == 0 and seq // QB >= 3
    assert seq % TM_FFN == 0 and seq % TM_IN == 0 and n_ctx % TQ_CONV == 0
    assert (n_b * n_ctx) % min(TM_FFN, n_b * n_ctx) == 0 and (n_b * n_ctx) % FFN_ROW_CHUNK == 0
    ctx_row = n_b

    cc = jnp.concatenate([c, c_ctx[None, :], jnp.zeros((MOD_ROWS - n_b - 1, d), F32)], axis=0)
    mod = _mod_table(cc, w_mod, b_mod)

    vec3 = lambda a: a.reshape(depth, 1, a.shape[-1])
    norm_ffn1, norm_mix, norm_ffn2 = vec3(norm_ffn1), vec3(norm_mix), vec3(norm_ffn2)
    conv_db, conv_ln_g, conv_ln_b = vec3(conv_db), vec3(conv_ln_g), vec3(conv_ln_b)
    gmlp_ln_g, gmlp_ln_b = vec3(gmlp_ln_g), vec3(gmlp_ln_b)
    bst = jnp.swapaxes(gmlp_bs, 1, 2)
    ws = gmlp_ws.astype(BF16)
    w1gu, w1d = ffn1_w_gu.astype(BF16), ffn1_w_down.astype(BF16)
    w2gu, w2d = ffn2_w_gu.astype(BF16), ffn2_w_down.astype(BF16)
    w_in_b = w_in.astype(BF16)
    wc, wg, wa, wo = (w.astype(BF16) for w in (w_conv_out, w_gmlp_out, w_attn_out, w_out))
    assert 2 * GRID_W == V7X_LANES and QB_ROWS % 2 == 0
    pairs, plan = _attn_bias_plan(seq // GRID_W)
    plan = jnp.asarray(plan)
    rpb_flat = attn_rpb.reshape(depth, -1)
    final_g = final_norm.reshape(1, d)

    xs = x.reshape(n_b * seq, d)
    cs = ctx.reshape(n_b * n_ctx, d)
    lat = dict(seq=seq, ctx_row=ctx_row)
    cxt = dict(seq=None, ctx_row=ctx_row)

    for i in range(depth):
        last = i == depth - 1
        xs, hx = _ffn(xs, mod, i, 0, norm_ffn1, w1gu, w1d, mix_g=norm_mix, **lat)
        cs, hc = _ffn(cs, mod, i, 0, norm_ffn1, w1gu, w1d, mix_g=norm_mix, **cxt)
        a_x, gm_x, qkv_x, gate_x = _inproj(hx, i, w_in_b, gmlp_ln_g, gmlp_ln_b, ws, bst)
        a_c, gm_c, qkv_c, gate_c = _inproj(hc, i, w_in_b, gmlp_ln_g, gmlp_ln_b, ws, bst)
        qkv_c3 = qkv_c.reshape(n_b, n_ctx, 3 * ATTN_DIM)
        table = _bias_table(rpb_flat[i], pairs)
        attn_x = _nbr_attn(qkv_x.reshape(n_b, seq, 3 * ATTN_DIM), qkv_c3, table, plan)
        conv_x = _conv(a_x, i, conv_dw, conv_db, conv_ln_g, conv_ln_b, seq=seq)
        xs = _outproj(conv_x, gm_x, attn_x.reshape(n_b * seq, ATTN_DIM), gate_x, xs, mod, i,
                      wc, wg, wa, wo, **lat)
        xs = _ffn(xs, mod, i, 2, norm_ffn2, w2gu, w2d, final_g=final_g if last else None, **lat)
        if not last:
            attn_c = _ctx_attn(qkv_c3)
            conv_c = _conv(a_c, i, conv_dw, conv_db, conv_ln_g, conv_ln_b, seq=n_ctx)
            cs = _outproj(conv_c, gm_c, attn_c.reshape(n_b * n_ctx, ATTN_DIM), gate_c, cs, mod, i,
                          wc, wg, wa, wo, **cxt)
            cs = _ffn(cs, mod, i, 2, norm_ffn2, w2gu, w2d, **cxt)
    return xs.reshape(n_b, seq, d)
```
